```python
import jax, jax.numpy as jnp
from jax import lax
import numpy as np

D_MODEL = 1024
BATCH = 1
SEQ = 16384
DEPTH = 4

GRID_W = 64
CTX_LEN = 256
N_MIXERS = 3
N_MOD = 9
D_FF = 2816
FFN_RES = 0.5
NORM_EPS = 1e-6

RW_HEAD = 64
RW_HEADS = D_MODEL // RW_HEAD
RW_DECAY_LORA = 64
RW_ICLR_LORA = 64
RW_VALUE_LORA = 32
RW_GATE_LORA = 160
RW_GN_EPS = 64e-5

MLA_HEADS = 16
MLA_NOPE = 64
MLA_ROPE = 32
MLA_V = 64
MLA_Q_RANK = 384
MLA_KV_RANK = 256
ROPE_BASE = 10000.0
Q_BLOCK = 128

POOL_WINDOWS = (2, 4, 8, 16)
POOL_GROUP = D_MODEL // len(POOL_WINDOWS)

N_RWKV = (DEPTH + 2) // 3
N_RWKV_VRES = N_RWKV - 1
N_MLA = (DEPTH + 1) // 3
N_POOL = DEPTH // 3

kernel_name = 'hybrid_rwkv7_mla_pool_dit_trunk'


def _rms(x, g, eps=NORM_EPS):
    xf = x.astype(jnp.float32)
    y = xf * lax.rsqrt(jnp.mean(xf * xf, axis=-1, keepdims=True) + eps)
    return (y * g.astype(jnp.float32)).astype(x.dtype)


def _swiglu(h, wg, wu, wd):
    return (jax.nn.silu(h @ wg) * (h @ wu)) @ wd


def _pre(s, mod, slot, g_pre):
    return _rms(s, g_pre) * (1 + mod[:, 3 * slot + 1]) + mod[:, 3 * slot]


def _post(s, y, mod, slot, g_post, weight):
    return s + weight * mod[:, 3 * slot + 2] * _rms(y, g_post)


def _ffn_sublayer(s, mod, slot, g_pre, g_post, wg, wu, wd):
    return _post(s, _swiglu(_pre(s, mod, slot, g_pre), wg, wu, wd), mod, slot, g_post, FFN_RES)


def _axial_angles(rows):
    row = jnp.repeat(jnp.arange(rows, dtype=jnp.float32), GRID_W)
    col = jnp.tile(jnp.arange(GRID_W, dtype=jnp.float32), rows)
    axis_dim = MLA_ROPE // 2
    inv = ROPE_BASE ** (-jnp.arange(0, axis_dim, 2, dtype=jnp.float32) / axis_dim)
    return row[:, None] * inv, col[:, None] * inv


def _rotate_half(x, ang):
    shape = (1, ang.shape[0]) + (1,) * (x.ndim - 3) + (ang.shape[1],)
    cos = jnp.cos(ang).reshape(shape)
    sin = jnp.sin(ang).reshape(shape)
    x1, x2 = jnp.split(x.astype(jnp.float32), 2, axis=-1)
    return jnp.concatenate([x1 * cos - x2 * sin, x1 * sin + x2 * cos], axis=-1).astype(x.dtype)


def _axial_rope(x, ang_row, ang_col):
    half = x.shape[-1] // 2
    return jnp.concatenate([_rotate_half(x[..., :half], ang_row),
                            _rotate_half(x[..., half:], ang_col)], axis=-1)


def _heads(t):
    return t.reshape(t.shape[:-1] + (RW_HEADS, RW_HEAD))


def _rwkv_stream(h, p, v_first, readout):
    B, T, D = h.shape
    zero = jnp.zeros((B, 1, D), h.dtype)
    prev = jnp.concatenate([zero, h[:, :-1]], axis=1)
    nxt = jnp.concatenate([h[:, 1:], zero], axis=1)
    xx = 0.5 * (prev + nxt) - h
    mu = p['mu']
    xw, xk, xv, xa = (h + xx * mu[n] for n in (1, 2, 3, 4))
    k = xk @ p['w_k']
    v = xv @ p['w_v']
    if v_first is None:
        v_first = v
    else:
        v = v + (v_first - v) * jax.nn.sigmoid(p['v0'] + (xv @ p['v1']) @ p['v2'])
    kkf = _heads(k * p['k_k']).astype(jnp.float32)
    kk = kkf * lax.rsqrt(jnp.maximum(jnp.sum(kkf * kkf, -1, keepdims=True), 1e-24))
    dirs = []
    for d in range(2):
        lw = (p['w0'][d] + jnp.tanh(xw @ p['w1'][d]) @ p['w2'][d]).astype(jnp.float32)
        decay = jnp.exp(-jnp.exp(-jax.nn.softplus(-lw) - 0.5))
        a = jax.nn.sigmoid(p['a0'][d] + (xa @ p['a1'][d]) @ p['a2'][d])
        k_d = k * (1 + (a - 1) * p['k_a'])
        dirs.append((_heads(decay), _heads(a).astype(jnp.float32), _heads(k_d).astype(jnp.float32)))
    st = {'v': _heads(v).astype(jnp.float32), 'kk': kk, 'dirs': dirs}
    if readout:
        xr = h + xx * mu[0]
        xg = h + xx * mu[5]
        st['r'] = _heads(xr @ p['w_r']).astype(jnp.float32)
        st['g'] = jax.nn.sigmoid(xg @ p['g1']) @ p['g2']
    return st, v_first


def _wkv_scan(s0, decay, k, v, kk, a, r, reverse):
    emit = r is not None
    seq = (decay, k, v, kk, a) + ((r,) if emit else ())
    xs = tuple(jnp.moveaxis(t, 1, 0) for t in seq)

    def step(s, inp):
        w_t, k_t, v_t, kk_t, a_t = inp[:5]
        s_kk = jnp.einsum('bhvk,bhk->bhv', s, kk_t)
        s = (s * w_t[:, :, None, :]
             - s_kk[..., None] * (kk_t * a_t)[:, :, None, :]
             + v_t[..., None] * k_t[:, :, None, :])
        return s, (jnp.einsum('bhvk,bhk->bhv', s, inp[5]) if emit else None)

    s_fin, ys = lax.scan(step, s0, xs, reverse=reverse)
    return s_fin, (jnp.moveaxis(ys, 0, 1) if emit else None)


def _rwkv_readout(st, ys, p, out_dtype):
    r, v = st['r'], st['v']
    B, T = r.shape[:2]
    ln_w = p['ln_w'].astype(jnp.float32)
    ln_b = p['ln_b'].astype(jnp.float32)
    r_k = p['r_k'].astype(jnp.float32)
    o = jnp.zeros((B, T, D_MODEL), jnp.float32)
    for (_, _, k_d), y in zip(st['dirs'], ys):
        mu = jnp.mean(y, -1, keepdims=True)
        var = jnp.mean(jnp.square(y - mu), -1, keepdims=True)
        yn = ((y - mu) * lax.rsqrt(var + RW_GN_EPS)).reshape(B, T, D_MODEL) * ln_w + ln_b
        bonus = (jnp.sum(r * k_d * r_k, -1, keepdims=True) * v).reshape(B, T, D_MODEL)
        o = o + yn + bonus
    return (o.astype(out_dtype) * st['g']) @ p['w_o']


def _rwkv_mixer(hc, hx, vf_c, vf_x, p, need_ctx):
    st_c, vf_c = _rwkv_stream(hc, p, vf_c, need_ctx)
    st_x, vf_x = _rwkv_stream(hx, p, vf_x, True)
    B = hx.shape[0]
    ys_c, ys_x = [], []
    for d, reverse in enumerate((False, True)):
        s0 = jnp.zeros((B, RW_HEADS, RW_HEAD, RW_HEAD), jnp.float32)
        dc, ac, kc = st_c['dirs'][d]
        s_ctx, y_c = _wkv_scan(s0, dc, kc, st_c['v'], st_c['kk'], ac, st_c.get('r'), reverse)
        dx, ax, kx = st_x['dirs'][d]
        _, y_x = _wkv_scan(s_ctx, dx, kx, st_x['v'], st_x['kk'], ax, st_x['r'], reverse)
        ys_c.append(y_c)
        ys_x.append(y_x)
    yx = _rwkv_readout(st_x, ys_x, p, hx.dtype)
    yc = _rwkv_readout(st_c, ys_c, p, hc.dtype) if need_ctx else None
    return yc, yx, vf_c, vf_x


def _mla_project(h, w_dq, q_norm, w_uq, w_dkv, kv_norm, w_ukv):
    B, T, _ = h.shape
    q = (_rms(h @ w_dq, q_norm) @ w_uq).reshape(B, T, MLA_HEADS, MLA_NOPE + MLA_ROPE)
    ckv = h @ w_dkv
    kv = (_rms(ckv[..., :MLA_KV_RANK], kv_norm) @ w_ukv).reshape(B, T, MLA_HEADS, MLA_NOPE + MLA_V)
    return (q[..., :MLA_NOPE], q[..., MLA_NOPE:], kv[..., :MLA_NOPE],
            ckv[..., MLA_KV_RANK:], kv[..., MLA_NOPE:])


def _attend(q_nope, q_rope, k_nope, k_rope, v):
    scale = (MLA_NOPE + MLA_ROPE) ** -0.5
    s = (jnp.einsum('bqhd,bkhd->bhqk', q_nope, k_nope)
         + jnp.einsum('bqhr,bkr->bhqk', q_rope, k_rope))
    p = jax.nn.softmax(s.astype(jnp.float32) * scale, axis=-1).astype(v.dtype)
    return jnp.einsum('bhqk,bkhd->bqhd', p, v)


def _blocked_attend(q_nope, q_rope, k_nope, k_rope, v):
    B, T = q_nope.shape[:2]
    nblk = T // Q_BLOCK

    def to_blocks(t):
        return jnp.moveaxis(t.reshape((B, nblk, Q_BLOCK) + t.shape[2:]), 1, 0)

    o = lax.map(lambda qs: _attend(qs[0], qs[1], k_nope, k_rope, v),
                (to_blocks(q_nope), to_blocks(q_rope)))
    return jnp.moveaxis(o, 0, 1).reshape((B, T) + o.shape[3:])


def _mla_mixer(hc, hx, ang_row, ang_col, w_dq, q_norm, w_uq, w_dkv, kv_norm, w_ukv, w_o, need_ctx):
    qn_c, qr_c, kn_c, kr_c, v_c = _mla_project(hc, w_dq, q_norm, w_uq, w_dkv, kv_norm, w_ukv)
    qn_x, qr_x, kn_x, kr_x, v_x = _mla_project(hx, w_dq, q_norm, w_uq, w_dkv, kv_norm, w_ukv)
    qr_x = _axial_rope(qr_x, ang_row, ang_col)
    kr_x = _axial_rope(kr_x, ang_row, ang_col)
    kn = jnp.concatenate([kn_c, kn_x], axis=1)
    kr = jnp.concatenate([kr_c, kr_x], axis=1)
    vv = jnp.concatenate([v_c, v_x], axis=1)
    B, T, _ = hx.shape
    yx = _blocked_attend(qn_x, qr_x, kn, kr, vv).reshape(B, T, MLA_HEADS * MLA_V) @ w_o
    yc = None
    if need_ctx:
        yc = _attend(qn_c, qr_c, kn_c, kr_c, v_c).reshape(B, hc.shape[1], MLA_HEADS * MLA_V) @ w_o
    return yc, yx


def _pool_mixer(h, w, b, scale):
    B, T, D = h.shape
    hf = h.astype(jnp.float32)
    csum = jnp.concatenate([jnp.zeros((B, 1, D), jnp.float32), jnp.cumsum(hf, axis=1)], axis=1)
    t = jnp.arange(T)
    outs = []
    for gi, win in enumerate(POOL_WINDOWS):
        lo = jnp.clip(t - win // 2, 0, T)
        hi = jnp.clip(t + win // 2, 0, T)
        sl = slice(gi * POOL_GROUP, (gi + 1) * POOL_GROUP)
        cg = csum[..., sl]
        mean = (jnp.take(cg, hi, axis=1) - jnp.take(cg, lo, axis=1)) / (hi - lo).astype(jnp.float32)[None, :, None]
        diff = (mean - hf[..., sl]).astype(h.dtype)
        outs.append(diff @ w[gi] + b[gi])
    return jnp.concatenate(outs, axis=-1) * scale


def setup_inputs(seed: int = 0) -> dict:
    key = jax.random.key(seed)
    ks = iter(jax.random.split(key, 64))
    D, F, H, N = D_MODEL, D_FF, RW_HEADS, RW_HEAD

    def nrm(shape, scale=1.0):
        return scale * jax.random.normal(next(ks), shape, jnp.float32)

    def gain(shape):
        return 1.0 + nrm(shape, 0.05)

    return {
        'x': nrm((BATCH, SEQ, D)),
        'c': nrm((BATCH, D)),
        'ctx': nrm((BATCH, CTX_LEN, D)),
        'c_ctx': nrm((D,)),
        'mod_w': nrm((DEPTH, D, N_MOD * D), 0.5 * D ** -0.5),
        'mod_b': nrm((DEPTH, N_MOD * D), 0.02),
        'norm_pre': gain((DEPTH, 3, D)),
        'norm_post': gain((DEPTH, 3, D)),
        'ffn_w_gate': nrm((DEPTH, 2, D, F), D ** -0.5),
        'ffn_w_up': nrm((DEPTH, 2, D, F), D ** -0.5),
        'ffn_w_down': nrm((DEPTH, 2, F, D), F ** -0.5),
        'rw_mu': jax.random.uniform(next(ks), (N_RWKV, 6, D), jnp.float32),
        'rw_w_r': nrm((N_RWKV, D, D), D ** -0.5),
        'rw_w_k': nrm((N_RWKV, D, D), D ** -0.5),
        'rw_w_v': nrm((N_RWKV, D, D), D ** -0.5),
        'rw_w_o': nrm((N_RWKV, D, D), D ** -0.5),
        'rw_w0': nrm((N_RWKV, 2, D)) - 3.0,
        'rw_w1': nrm((N_RWKV, 2, D, RW_DECAY_LORA), D ** -0.5),
        'rw_w2': nrm((N_RWKV, 2, RW_DECAY_LORA, D), 0.5 * RW_DECAY_LORA ** -0.5),
        'rw_a0': nrm((N_RWKV, 2, D), 0.5),
        'rw_a1': nrm((N_RWKV, 2, D, RW_ICLR_LORA), D ** -0.5),
        'rw_a2': nrm((N_RWKV, 2, RW_ICLR_LORA, D), 0.5 * RW_ICLR_LORA ** -0.5),
        'rw_v0': nrm((N_RWKV_VRES, D), 0.5),
        'rw_v1': nrm((N_RWKV_VRES, D, RW_VALUE_LORA), D ** -0.5),
        'rw_v2': nrm((N_RWKV_VRES, RW_VALUE_LORA, D), 0.5 * RW_VALUE_LORA ** -0.5),
        'rw_g1': nrm((N_RWKV, D, RW_GATE_LORA), D ** -0.5),
        'rw_g2': nrm((N_RWKV, RW_GATE_LORA, D), RW_GATE_LORA ** -0.5),
        'rw_k_k': 0.85 + nrm((N_RWKV, D), 0.05),
        'rw_k_a': 1.0 + nrm((N_RWKV, D), 0.05),
        'rw_r_k': nrm((N_RWKV, H, N), 0.1),
        'rw_ln_w': gain((N_RWKV, D)),
        'rw_ln_b': nrm((N_RWKV, D), 0.01),
        'mla_w_dq': nrm((N_MLA, D, MLA_Q_RANK), D ** -0.5),
        'mla_q_norm': gain((N_MLA, MLA_Q_RANK)),
        'mla_w_uq': nrm((N_MLA, MLA_Q_RANK, MLA_HEADS * (MLA_NOPE + MLA_ROPE)), MLA_Q_RANK ** -0.5),
        'mla_w_dkv': nrm((N_MLA, D, MLA_KV_RANK + MLA_ROPE), D ** -0.5),
        'mla_kv_norm': gain((N_MLA, MLA_KV_RANK)),
        'mla_w_ukv': nrm((N_MLA, MLA_KV_RANK, MLA_HEADS * (MLA_NOPE + MLA_V)), MLA_KV_RANK ** -0.5),
        'mla_w_o': nrm((N_MLA, MLA_HEADS * MLA_V, D), (MLA_HEADS * MLA_V) ** -0.5),
        'pool_w': nrm((N_POOL, len(POOL_WINDOWS), POOL_GROUP, POOL_GROUP), POOL_GROUP ** -0.5),
        'pool_b': nrm((N_POOL, len(POOL_WINDOWS), POOL_GROUP), 0.01),
        'pool_scale': 1.0 + nrm((N_POOL, D), 0.1),
    }


def reference(x, c, ctx, c_ctx, mod_w, mod_b, norm_pre, norm_post, ffn_w_gate, ffn_w_up, ffn_w_down,
              rw_mu, rw_w_r, rw_w_k, rw_w_v, rw_w_o, rw_w0, rw_w1, rw_w2, rw_a0, rw_a1, rw_a2,
              rw_v0, rw_v1, rw_v2, rw_g1, rw_g2, rw_k_k, rw_k_a, rw_r_k, rw_ln_w, rw_ln_b,
              mla_w_dq, mla_q_norm, mla_w_uq, mla_w_dkv, mla_kv_norm, mla_w_ukv, mla_w_o,
              pool_w, pool_b, pool_scale):
    B, T, D = x.shape
    rows = T // GRID_W
    ang_row, ang_col = _axial_angles(rows)
    cs = ctx
    sc = jax.nn.silu(c)
    scc = jax.nn.silu(c_ctx)[None]
    vf_c = None
    vf_x = None
    for i in range(DEPTH):
        kind, j = i % N_MIXERS, i // N_MIXERS
        last = i == DEPTH - 1
        ctx_live = (not last) or kind != 2
        mod_x = (sc @ mod_w[i] + mod_b[i]).reshape(B, N_MOD, 1, D)
        mod_c = (scc @ mod_w[i] + mod_b[i]).reshape(1, N_MOD, 1, D)

        x = _ffn_sublayer(x, mod_x, 0, norm_pre[i, 0], norm_post[i, 0],
                          ffn_w_gate[i, 0], ffn_w_up[i, 0], ffn_w_down[i, 0])
        if ctx_live:
            cs = _ffn_sublayer(cs, mod_c, 0, norm_pre[i, 0], norm_post[i, 0],
                               ffn_w_gate[i, 0], ffn_w_up[i, 0], ffn_w_down[i, 0])

        hx = _pre(x, mod_x, 1, norm_pre[i, 1])
        hc = _pre(cs, mod_c, 1, norm_pre[i, 1]) if ctx_live else None
        if kind == 0:
            p = {'mu': rw_mu[j], 'w_r': rw_w_r[j], 'w_k': rw_w_k[j], 'w_v': rw_w_v[j], 'w_o': rw_w_o[j],
                 'w0': rw_w0[j], 'w1': rw_w1[j], 'w2': rw_w2[j],
                 'a0': rw_a0[j], 'a1': rw_a1[j], 'a2': rw_a2[j],
                 'g1': rw_g1[j], 'g2': rw_g2[j], 'k_k': rw_k_k[j], 'k_a': rw_k_a[j], 'r_k': rw_r_k[j],
                 'ln_w': rw_ln_w[j], 'ln_b': rw_ln_b[j]}
            if j > 0:
                p['v0'] = rw_v0[j - 1]
                p['v1'] = rw_v1[j - 1]
                p['v2'] = rw_v2[j - 1]
            yc, yx, vf_c, vf_x = _rwkv_mixer(hc, hx, vf_c, vf_x, p, not last)
        elif kind == 1:
            yc, yx = _mla_mixer(hc, hx, ang_row, ang_col, mla_w_dq[j], mla_q_norm[j], mla_w_uq[j],
                                mla_w_dkv[j], mla_kv_norm[j], mla_w_ukv[j], mla_w_o[j], not last)
        else:
            yx = _pool_mixer(hx, pool_w[j], pool_b[j], pool_scale[j])
            yc = _pool_mixer(hc, pool_w[j], pool_b[j], pool_scale[j]) if not last else None
        x = _post(x, yx, mod_x, 1, norm_post[i, 1], 1.0)

        x = _ffn_sublayer(x, mod_x, 2, norm_pre[i, 2], norm_post[i, 2],
                          ffn_w_gate[i, 1], ffn_w_up[i, 1], ffn_w_down[i, 1])
        if not last:
            cs = _post(cs, yc, mod_c, 1, norm_post[i, 1], 1.0)
            cs = _ffn_sublayer(cs, mod_c, 2, norm_pre[i, 2], norm_post[i, 2],
                               ffn_w_gate[i, 1], ffn_w_up[i, 1], ffn_w_down[i, 1])
    return x
```

```python
import functools
import math

import jax
import jax.numpy as jnp
from jax import lax
from jax.experimental import pallas as pl
from jax.experimental.pallas import tpu as pltpu

F32 = jnp.float32
BF16 = jnp.bfloat16
HIGHEST = lax.Precision.HIGHEST

D_MODEL = 1024
DEPTH = 4
CTX_LEN = 256
GRID_W = 64
N_MOD = 9
FFN_RES = 0.5
NORM_EPS = 1e-6

RW_HEAD = 64
RW_GN_EPS = 64e-5
RW_GATE_LORA_PAD = 256
LORA_PAD = 128

MLA_HEADS = 16
MLA_NOPE = 64
MLA_ROPE = 32
MLA_V = 64
MLA_Q_RANK = 384
MLA_KV_RANK = 256
ROPE_BASE = 10000.0
MLA_QK_PAD = 128

POOL_WINDOWS = (2, 4, 8, 16)
POOL_GROUP = D_MODEL // len(POOL_WINDOWS)
POOL_HALO = 8

LANES = 128
SUBLANES = 8
ROW_TILE = 256
WKV_CHUNK = 64
VMEM_LIMIT = 56 * 1024 * 1024


def _cparams(sem):
    return pltpu.CompilerParams(dimension_semantics=sem, vmem_limit_bytes=VMEM_LIMIT)


def _full(a):
    nd = a.ndim
    return pl.BlockSpec(a.shape, lambda *_: (0,) * nd)


def _sigmoid(x):
    return 1.0 / (1.0 + jnp.exp(-x))


def _rms(x, g):
    return x * lax.rsqrt(jnp.mean(x * x, axis=-1, keepdims=True) + NORM_EPS) * g


def _pre(s, mod_ref, slot, gpre_ref):
    shift = mod_ref[0, 3 * slot:3 * slot + 1, :]
    scale = mod_ref[0, 3 * slot + 1:3 * slot + 2, :]
    return _rms(s, gpre_ref[slot:slot + 1, :]) * (1.0 + scale) + shift


def _post(s, y, mod_ref, slot, gpost_ref, weight):
    gate = mod_ref[0, 3 * slot + 2:3 * slot + 3, :]
    return s + (weight * gate) * _rms(y, gpost_ref[slot:slot + 1, :])


def _bdot(a, b):
    return jnp.dot(a.astype(BF16), b.astype(BF16), preferred_element_type=F32)


def _row_spec(width=D_MODEL):
    return pl.BlockSpec((ROW_TILE, width), lambda i: (i, 0))


def _mod_spec(n_rows):
    nxt = (n_rows - CTX_LEN) // ROW_TILE
    return pl.BlockSpec((1, N_MOD, D_MODEL), lambda i: (jnp.where(i >= nxt, 0, 1), 0, 0))


def _halo_specs(n_rows):
    per = ROW_TILE // SUBLANES
    nblk = n_rows // SUBLANES
    prev = pl.BlockSpec((SUBLANES, D_MODEL), lambda i: (jnp.maximum(i * per - 1, 0), 0))
    nxt = pl.BlockSpec((SUBLANES, D_MODEL), lambda i: (jnp.minimum((i + 1) * per, nblk - 1), 0))
    return prev, nxt


def _stream_edges():
    i = pl.program_id(0)
    nt = pl.num_programs(0)
    nxt = nt - CTX_LEN // ROW_TILE
    first = jnp.logical_or(i == 0, i == nxt)
    last = jnp.logical_or(i == nxt - 1, i == nt - 1)
    return first, last


def _mod_body(c_ref, w_ref, b_ref, o_ref):
    c = c_ref[...]
    s = c * _sigmoid(c)
    o_ref[0] = _bdot(s, w_ref[0]) + b_ref[0]


def _modulation(c2, mod_w, mod_b):
    tn = D_MODEL
    out = pl.pallas_call(
        _mod_body,
        grid=(DEPTH, N_MOD * D_MODEL // tn),
        in_specs=[pl.BlockSpec((SUBLANES, D_MODEL), lambda l, j: (0, 0)),
                  pl.BlockSpec((1, D_MODEL, tn), lambda l, j: (l, 0, j)),
                  pl.BlockSpec((1, 1, tn), lambda l, j: (l, 0, j))],
        out_specs=pl.BlockSpec((1, SUBLANES, tn), lambda l, j: (l, 0, j)),
        out_shape=jax.ShapeDtypeStruct((DEPTH, SUBLANES, N_MOD * D_MODEL), F32),
        compiler_params=_cparams(("arbitrary", "arbitrary")),
        name="modulation",
    )(c2, mod_w, mod_b.reshape(DEPTH, 1, N_MOD * D_MODEL))
    return out[:, :2].reshape(DEPTH, 2, N_MOD, D_MODEL)


def _ffn_body(s_ref, mod_ref, gpre_ref, gpost_ref, wg_ref, wu_ref, wd_ref, o_ref, *, slot):
    s = s_ref[...]
    h = _pre(s, mod_ref, slot, gpre_ref).astype(BF16)
    g = jnp.dot(h, wg_ref[...], preferred_element_type=F32)
    u = jnp.dot(h, wu_ref[...], preferred_element_type=F32)
    act = (g * _sigmoid(g)) * u
    y = jnp.dot(act.astype(BF16), wd_ref[...], preferred_element_type=F32)
    o_ref[...] = _post(s, y, mod_ref, slot, gpost_ref, FFN_RES)


def _ffn(s, mod, gpre, gpost, wg, wu, wd, slot):
    n = s.shape[0]
    return pl.pallas_call(
        functools.partial(_ffn_body, slot=slot),
        grid=(n // ROW_TILE,),
        in_specs=[_row_spec(), _mod_spec(n), _full(gpre), _full(gpost),
                  _full(wg), _full(wu), _full(wd)],
        out_specs=_row_spec(),
        out_shape=jax.ShapeDtypeStruct(s.shape, F32),
        compiler_params=_cparams(("arbitrary",)),
        name="ffn",
    )(s, mod, gpre, gpost, wg, wu, wd)


def _rwkv_proj_body(*refs, vres):
    (s_ref, sp_ref, sn_ref, mod_ref, gpre_ref, mu_ref, wr_ref, wk_ref, wv_ref,
     w0_ref, w1_ref, w2_ref, a0_ref, a1_ref, a2_ref, g1_ref, g2_ref) = refs[:17]
    refs = refs[17:]
    if vres:
        v0_ref, v1_ref, v2_ref, vf_ref = refs[:4]
        refs = refs[4:]
    r_ref, k_ref, v_ref, g_ref, lw_ref, a_ref = refs

    first, last = _stream_edges()
    h = _pre(s_ref[...], mod_ref, 1, gpre_ref)
    hp = _pre(sp_ref[SUBLANES - 1:SUBLANES, :], mod_ref, 1, gpre_ref)
    hn = _pre(sn_ref[0:1, :], mod_ref, 1, gpre_ref)
    hp = jnp.where(first, 0.0, hp)
    hn = jnp.where(last, 0.0, hn)
    row = lax.broadcasted_iota(jnp.int32, h.shape, 0)
    prev = jnp.where(row == 0, hp, pltpu.roll(h, 1, 0))
    nxt = jnp.where(row == ROW_TILE - 1, hn, pltpu.roll(h, ROW_TILE - 1, 0))
    xx = 0.5 * (prev + nxt) - h

    def mix(n):
        return h + xx * mu_ref[n:n + 1, :]

    r_ref[...] = _bdot(mix(0), wr_ref[...])
    k_ref[...] = _bdot(mix(2), wk_ref[...])
    xvb = mix(3).astype(BF16)
    v = jnp.dot(xvb, wv_ref[...], preferred_element_type=F32)
    if vres:
        gate = _sigmoid(v0_ref[...] + _bdot(jnp.dot(xvb, v1_ref[...], preferred_element_type=F32),
                                            v2_ref[...]))
        v = v + (vf_ref[...] - v) * gate
    v_ref[...] = v
    g_ref[...] = _bdot(_sigmoid(_bdot(mix(5), g1_ref[...])), g2_ref[...])
    lw = w0_ref[...] + _bdot(jnp.tanh(_bdot(mix(1), w1_ref[...])), w2_ref[...])
    logw = -math.exp(-0.5) * _sigmoid(lw)
    aa = _sigmoid(a0_ref[...] + _bdot(_bdot(mix(4), a1_ref[...]), a2_ref[...]))
    for d in range(2):
        lw_ref[d] = logw[:, d * D_MODEL:(d + 1) * D_MODEL]
        a_ref[d] = aa[:, d * D_MODEL:(d + 1) * D_MODEL]


def _lora_pair(w1, w2, w0):
    lora = w1.shape[-1]
    w1c = jnp.concatenate([w1[0], w1[1]], axis=1)
    z = jnp.zeros((lora, D_MODEL), w2.dtype)
    w2c = jnp.concatenate([jnp.concatenate([w2[0], z], axis=1),
                           jnp.concatenate([z, w2[1]], axis=1)], axis=0)
    return w1c.astype(BF16), w2c.astype(BF16), w0.reshape(1, 2 * D_MODEL)


def _pad_to(a, axis, size):
    pad = [(0, 0)] * a.ndim
    pad[axis] = (0, size - a.shape[axis])
    return jnp.pad(a, pad)


def _rwkv_proj(s, mod, gpre, p, v_first):
    n = s.shape[0]
    vres = v_first is not None
    w1c, w2c, w0c = _lora_pair(p['w1'], p['w2'], p['w0'])
    a1c, a2c, a0c = _lora_pair(p['a1'], p['a2'], p['a0'])
    g1 = _pad_to(p['g1'], 1, RW_GATE_LORA_PAD).astype(BF16)
    g2 = _pad_to(p['g2'], 0, RW_GATE_LORA_PAD).astype(BF16)
    prev_spec, next_spec = _halo_specs(n)
    args = [s, s, s, mod, gpre, p['mu'], p['w_r'].astype(BF16), p['w_k'].astype(BF16),
            p['w_v'].astype(BF16), w0c, w1c, w2c, a0c, a1c, a2c, g1, g2]
    specs = [_row_spec(), prev_spec, next_spec, _mod_spec(n)] + [_full(a) for a in args[4:]]
    if vres:
        extra = [p['v0'].reshape(1, D_MODEL), _pad_to(p['v1'], 1, LORA_PAD).astype(BF16),
                 _pad_to(p['v2'], 0, LORA_PAD).astype(BF16)]
        args += extra + [v_first]
        specs += [_full(a) for a in extra] + [_row_spec()]
    row_out = jax.ShapeDtypeStruct((n, D_MODEL), F32)
    dir_out = jax.ShapeDtypeStruct((2, n, D_MODEL), F32)
    dir_spec = pl.BlockSpec((2, ROW_TILE, D_MODEL), lambda i: (0, i, 0))
    return pl.pallas_call(
        functools.partial(_rwkv_proj_body, vres=vres),
        grid=(n // ROW_TILE,),
        in_specs=specs,
        out_specs=[_row_spec()] * 4 + [dir_spec] * 2,
        out_shape=[row_out] * 4 + [dir_out] * 2,
        compiler_params=_cparams(("arbitrary",)),
        name="rwkv_proj",
    )(*args)


WKV_PAIR = LANES // RW_HEAD
WKV_ROWS = WKV_PAIR * WKV_CHUNK
WKV_GROUPS = D_MODEL // LANES
WKV_NEUMANN_LEVELS = 5


def _wkv_masks(sgn):
    c = WKV_CHUNK
    ti = lax.broadcasted_iota(jnp.int32, (c, c), 0)
    si = lax.broadcasted_iota(jnp.int32, (c, c), 1)
    incl_c = ((ti - si) * sgn >= 0).astype(F32)
    rows = lax.broadcasted_iota(jnp.int32, (WKV_ROWS, LANES), 0)
    lanes = lax.broadcasted_iota(jnp.int32, (WKV_ROWS, LANES), 1)
    keep = (rows < c) == (lanes < RW_HEAD)
    tr = lax.broadcasted_iota(jnp.int32, (WKV_ROWS, WKV_ROWS), 0)
    sr = lax.broadcasted_iota(jnp.int32, (WKV_ROWS, WKV_ROWS), 1)
    same = (tr < c) == (sr < c)
    strict = jnp.logical_and(same, (tr - sr) * sgn > 0)
    incl = jnp.logical_and(same, (tr - sr) * sgn >= 0)
    eye = (tr == sr).astype(F32)
    return incl_c, keep, strict, incl, eye


def _dot_nt(x, y):
    return lax.dot_general(x.astype(BF16), y.astype(BF16), (((1,), (1,)), ((), ())),
                           preferred_element_type=F32)


def _pdot(x, y, prec):
    if prec is None:
        return _bdot(x, y)
    return jnp.dot(x, y, precision=prec, preferred_element_type=F32)


def _wkv_group(lw, a, k, v, r, k_k, k_a, r_k, ln_w, ln_b, a_state, masks, prec_inv, prec_state):
    c = WKV_CHUNK
    incl_c, keep, strict, incl, eye = masks
    head_a = lax.broadcasted_iota(jnp.int32, (c, LANES), 1) < RW_HEAD

    def head_sum(x):
        sa = jnp.sum(jnp.where(head_a, x, 0.0), axis=-1, keepdims=True)
        sb = jnp.sum(jnp.where(head_a, 0.0, x), axis=-1, keepdims=True)
        return jnp.where(head_a, sa, sb)

    kkf = k * k_k
    kap = kkf * lax.rsqrt(jnp.maximum(head_sum(kkf * kkf), 1e-24))
    kd = k * (1.0 + (a - 1.0) * k_a)
    b = kap * a

    cum = jnp.dot(incl_c, lw, precision=HIGHEST, preferred_element_type=F32)
    tot = jnp.sum(lw, axis=0, keepdims=True)
    e_neg = jnp.exp(-cum)
    e_rem = jnp.exp(tot - cum)
    kt = kap * jnp.exp(cum - lw)
    rt = r * jnp.exp(cum)
    g_c = jnp.exp(tot)

    def stack(x):
        return jnp.where(keep, jnp.concatenate([x, x], axis=0), 0.0)

    kt_s, rt_s, v_s = stack(kt), stack(rt), stack(v)
    kh_s, bh_s = stack(kd * e_neg), stack(b * e_neg)
    kb_s, bb_s = stack(kd * e_rem), stack(b * e_rem)

    r2 = WKV_ROWS
    prod = _dot_nt(jnp.concatenate([kt_s, rt_s], axis=0), jnp.concatenate([kh_s, bh_s], axis=0))
    l_k = jnp.where(strict, prod[:r2, :r2], 0.0)
    l_b = jnp.where(strict, prod[:r2, r2:], 0.0)
    p_k = jnp.where(incl, prod[r2:, :r2], 0.0)
    p_b = jnp.where(incl, prod[r2:, r2:], 0.0)

    nm = -l_b
    t_inv = eye + nm
    for _ in range(WKV_NEUMANN_LEVELS):
        nm = _pdot(nm, nm, prec_inv)
        t_inv = t_inv + _pdot(t_inv, nm, prec_inv)

    wu = _pdot(t_inv, jnp.concatenate([kt_s, _bdot(l_k, v_s)], axis=1), prec_inv)
    pbwu = _bdot(p_b, wu)
    q_m = rt_s - pbwu[:, :LANES]
    y_0 = _bdot(p_k, v_s) - pbwu[:, LANES:]
    bbwu = _bdot(bb_s.T, wu)
    g_m = eye * g_c - bbwu[:, :LANES]
    h_m = _bdot(kb_s.T, v_s) - bbwu[:, LANES:]

    y_s = _pdot(q_m, a_state, prec_state) + y_0
    a_new = _pdot(g_m, a_state, prec_state) + h_m
    y = y_s[:c] + y_s[c:]

    mu = head_sum(y) * (1.0 / RW_HEAD)
    yc = y - mu
    var = head_sum(yc * yc) * (1.0 / RW_HEAD)
    yn = yc * lax.rsqrt(var + RW_GN_EPS) * ln_w + ln_b
    bonus = head_sum(r * kd * r_k) * v
    return yn + bonus, a_new


def _wkv_body(lw_ref, a_ref, k_ref, v_ref, r_ref, kk_ref, ka_ref, rk_ref, lnw_ref, lnb_ref,
              o_ref, st_ref, *, prec_inv, prec_state):
    @pl.when(pl.program_id(1) == 0)
    def _():
        st_ref[...] = jnp.zeros_like(st_ref)

    masks = _wkv_masks(1 - 2 * pl.program_id(0))
    for g in range(WKV_GROUPS):
        sl = slice(g * LANES, (g + 1) * LANES)
        o, a_new = _wkv_group(lw_ref[0, :, sl], a_ref[0, :, sl], k_ref[:, sl], v_ref[:, sl],
                              r_ref[:, sl], kk_ref[:, sl], ka_ref[:, sl], rk_ref[:, sl],
                              lnw_ref[:, sl], lnb_ref[:, sl], st_ref[g], masks,
                              prec_inv, prec_state)
        o_ref[0, :, sl] = o
        st_ref[g] = a_new


def _wkv(logw, a, k, v, r, p, prec_inv=HIGHEST, prec_state=HIGHEST):
    n = k.shape[0]
    nc = n // WKV_CHUNK
    ncc = CTX_LEN // WKV_CHUNK
    ncx = nc - ncc

    def chunk(d, j):
        fwd = jnp.where(j < ncc, ncx + j, j - ncc)
        rev = jnp.where(j < ncc, nc - 1 - j, nc - 1 - j)
        return jnp.where(d == 0, fwd, rev)

    dir_spec = pl.BlockSpec((1, WKV_CHUNK, D_MODEL), lambda d, j: (d, chunk(d, j), 0))
    row_spec = pl.BlockSpec((WKV_CHUNK, D_MODEL), lambda d, j: (chunk(d, j), 0))
    vecs = [p['k_k'].reshape(1, D_MODEL), p['k_a'].reshape(1, D_MODEL),
            p['r_k'].reshape(1, D_MODEL), p['ln_w'].reshape(1, D_MODEL),
            p['ln_b'].reshape(1, D_MODEL)]
    return pl.pallas_call(
        functools.partial(_wkv_body, prec_inv=prec_inv, prec_state=prec_state),
        grid=(2, nc),
        in_specs=[dir_spec, dir_spec, row_spec, row_spec, row_spec] + [_full(x) for x in vecs],
        out_specs=dir_spec,
        out_shape=jax.ShapeDtypeStruct((2, n, D_MODEL), F32),
        scratch_shapes=[pltpu.VMEM((WKV_GROUPS, WKV_ROWS, LANES), F32)],
        compiler_params=_cparams(("arbitrary", "arbitrary")),
        name="wkv_scan",
    )(logw, a, k, v, r, *vecs)


def _out_body(*refs, gated):
    if gated:
        o_ref, g_ref, w_ref, s_ref, mod_ref, gpost_ref, out_ref = refs
        y = (o_ref[0] + o_ref[1]) * g_ref[...]
    else:
        o_ref, w_ref, s_ref, mod_ref, gpost_ref, out_ref = refs
        y = o_ref[...]
    y = jnp.dot(y.astype(BF16), w_ref[...], preferred_element_type=F32)
    out_ref[...] = _post(s_ref[...], y, mod_ref, 1, gpost_ref, 1.0)


def _mixer_out(o, g, w_o, s, mod, gpost):
    n = s.shape[0]
    gated = g is not None
    if gated:
        args = [o, g, w_o, s, mod, gpost]
        specs = [pl.BlockSpec((2, ROW_TILE, D_MODEL), lambda i: (0, i, 0)), _row_spec(),
                 _full(w_o), _row_spec(), _mod_spec(n), _full(gpost)]
    else:
        args = [o, w_o, s, mod, gpost]
        specs = [_row_spec(), _full(w_o), _row_spec(), _mod_spec(n), _full(gpost)]
    return pl.pallas_call(
        functools.partial(_out_body, gated=gated),
        grid=(n // ROW_TILE,),
        in_specs=specs,
        out_specs=_row_spec(),
        out_shape=jax.ShapeDtypeStruct(s.shape, F32),
        compiler_params=_cparams(("arbitrary",)),
        name="mixer_out",
    )(*args)


def _mla_proj_body(s_ref, mod_ref, gpre_ref, wdq_ref, qn_ref, wq1_ref, wq2_ref, wdkv_ref, kvn_ref,
                   wkr_ref, wk_ref, wv_ref, place_ref, tq_ref, tk_ref, q_ref, k_ref, v_ref):
    h = _pre(s_ref[...], mod_ref, 1, gpre_ref).astype(BF16)
    cq = jnp.dot(h, wdq_ref[...], preferred_element_type=F32)
    qn = _rms(cq, qn_ref[...]).astype(BF16)
    q1 = jnp.dot(qn, wq1_ref[...], preferred_element_type=F32)
    q2 = jnp.dot(qn, wq2_ref[...], preferred_element_type=F32)
    ckv = jnp.dot(h, wdkv_ref[...], preferred_element_type=F32)
    kvn = _rms(ckv, kvn_ref[...]).astype(BF16)
    kr = jnp.dot(h, wkr_ref[...], preferred_element_type=F32)
    krr = kr * tk_ref[0] + pltpu.roll(kr, LANES - MLA_ROPE, 1) * tk_ref[1]
    kk = (jnp.dot(kvn, wk_ref[...], preferred_element_type=F32)
          + jnp.dot(krr.astype(BF16), place_ref[...], preferred_element_type=F32))
    k_ref[...] = kk.astype(BF16)
    v_ref[...] = jnp.dot(kvn, wv_ref[...], preferred_element_type=F32).astype(BF16)
    cos = tq_ref[0]
    sin = tq_ref[1]
    for hd in range(MLA_HEADS):
        sl = slice(hd * MLA_QK_PAD, (hd + 1) * MLA_QK_PAD)
        q_ref[:, sl] = (q1[:, sl] * cos + q2[:, sl] * sin).astype(BF16)


def _rot_cols(w):
    half = MLA_ROPE // 4
    parts = []
    for ax in range(2):
        blk = w[..., ax * 2 * half:(ax + 1) * 2 * half]
        parts += [-blk[..., half:], blk[..., :half]]
    return jnp.concatenate(parts, axis=-1)


def _mla_tables(n):
    t = n - CTX_LEN
    pos = jnp.arange(t, dtype=jnp.int32)
    rowp = (pos // GRID_W).astype(F32)
    colp = (pos % GRID_W).astype(F32)
    axis_dim = MLA_ROPE // 2
    inv = ROPE_BASE ** (-jnp.arange(0, axis_dim, 2, dtype=F32) / axis_dim)
    ar = rowp[:, None] * inv
    ac = colp[:, None] * inv
    cosf = jnp.concatenate([jnp.cos(ar), jnp.cos(ar), jnp.cos(ac), jnp.cos(ac)], axis=-1)
    sinf = jnp.concatenate([jnp.sin(ar), jnp.sin(ar), jnp.sin(ac), jnp.sin(ac)], axis=-1)
    cosf = jnp.concatenate([cosf, jnp.ones((CTX_LEN, MLA_ROPE), F32)], axis=0)
    sinf = jnp.concatenate([sinf, jnp.zeros((CTX_LEN, MLA_ROPE), F32)], axis=0)
    scale = (MLA_NOPE + MLA_ROPE) ** -0.5
    zq = jnp.zeros((n, MLA_QK_PAD - MLA_NOPE - MLA_ROPE), F32)
    tq = jnp.stack([jnp.concatenate([jnp.ones((n, MLA_NOPE), F32), cosf, zq], axis=-1),
                    jnp.concatenate([jnp.zeros((n, MLA_NOPE), F32), sinf, zq], axis=-1)]) * scale
    zk = jnp.zeros((n, LANES - MLA_ROPE), F32)
    tk = jnp.stack([jnp.concatenate([cosf, zk], axis=-1), jnp.concatenate([sinf, zk], axis=-1)])
    return tq, tk


def _mla_proj(s, mod, gpre, w_dq, q_norm, w_uq, w_dkv, kv_norm, w_ukv):
    n = s.shape[0]
    hq = MLA_NOPE + MLA_ROPE
    wq = w_uq.reshape(MLA_Q_RANK, MLA_HEADS, hq)
    zq = jnp.zeros((MLA_Q_RANK, MLA_HEADS, MLA_QK_PAD - hq), w_uq.dtype)
    wq1 = jnp.concatenate([wq, zq], axis=-1).reshape(MLA_Q_RANK, MLA_HEADS * MLA_QK_PAD)
    wq2 = jnp.concatenate([jnp.zeros_like(wq[..., :MLA_NOPE]), _rot_cols(wq[..., MLA_NOPE:]), zq],
                          axis=-1).reshape(MLA_Q_RANK, MLA_HEADS * MLA_QK_PAD)
    wkv = w_ukv.reshape(MLA_KV_RANK, MLA_HEADS, MLA_NOPE + MLA_V)
    wk = jnp.concatenate([wkv[..., :MLA_NOPE],
                          jnp.zeros((MLA_KV_RANK, MLA_HEADS, MLA_QK_PAD - MLA_NOPE), w_ukv.dtype)],
                         axis=-1).reshape(MLA_KV_RANK, MLA_HEADS * MLA_QK_PAD)
    wv = wkv[..., MLA_NOPE:].reshape(MLA_KV_RANK, MLA_HEADS * MLA_V)
    w_rope = w_dkv[:, MLA_KV_RANK:]
    wkr = jnp.concatenate([w_rope, _rot_cols(w_rope),
                           jnp.zeros((D_MODEL, LANES - 2 * MLA_ROPE), w_dkv.dtype)], axis=-1)
    src = jnp.arange(LANES)[:, None]
    dst = jnp.arange(MLA_HEADS * MLA_QK_PAD)[None, :]
    place = jnp.logical_and(src < MLA_ROPE, dst % MLA_QK_PAD == MLA_NOPE + src).astype(BF16)
    tq, tk = _mla_tables(n)
    weights = [w_dq.astype(BF16), q_norm.reshape(1, MLA_Q_RANK), wq1.astype(BF16), wq2.astype(BF16),
               w_dkv[:, :MLA_KV_RANK].astype(BF16), kv_norm.reshape(1, MLA_KV_RANK),
               wkr.astype(BF16), wk.astype(BF16), wv.astype(BF16), place]
    tab_spec = pl.BlockSpec((2, ROW_TILE, LANES), lambda i: (0, i, 0))
    return pl.pallas_call(
        _mla_proj_body,
        grid=(n // ROW_TILE,),
        in_specs=[_row_spec(), _mod_spec(n), _full(gpre)] + [_full(w) for w in weights]
                 + [tab_spec, tab_spec],
        out_specs=[_row_spec(MLA_HEADS * MLA_QK_PAD), _row_spec(MLA_HEADS * MLA_QK_PAD),
                   _row_spec(MLA_HEADS * MLA_V)],
        out_shape=[jax.ShapeDtypeStruct((n, MLA_HEADS * MLA_QK_PAD), BF16),
                   jax.ShapeDtypeStruct((n, MLA_HEADS * MLA_QK_PAD), BF16),
                   jax.ShapeDtypeStruct((n, MLA_HEADS * MLA_V), BF16)],
        compiler_params=_cparams(("arbitrary",)),
        name="mla_proj",
    )(s, mod, gpre, *weights, tq, tk)


MLA_PAIR = LANES // MLA_V


def _attn_body(q_ref, k_ref, v_ref, o_ref, m_ref, l_ref, acc_ref, *, tk, nk):
    m_ref[...] = jnp.full_like(m_ref, -jnp.inf)
    l_ref[...] = jnp.zeros_like(l_ref)
    acc_ref[...] = jnp.zeros_like(acc_ref)

    def step(i, carry):
        rows = pl.ds(pl.multiple_of(i * tk, tk), tk)
        vv = v_ref[rows, :]
        for hd in range(MLA_PAIR):
            sl = slice(hd * MLA_QK_PAD, (hd + 1) * MLA_QK_PAD)
            s = lax.dot_general(q_ref[:, sl], k_ref[rows, sl], (((1,), (1,)), ((), ())),
                                preferred_element_type=F32)
            m_old = m_ref[hd]
            m_new = jnp.maximum(m_old, jnp.max(s, axis=-1, keepdims=True))
            alpha = jnp.exp(m_old - m_new)
            p = jnp.exp(s - m_new)
            l_ref[hd] = alpha * l_ref[hd] + jnp.sum(p, axis=-1, keepdims=True)
            acc_ref[hd] = alpha * acc_ref[hd] + jnp.dot(p.astype(BF16), vv,
                                                        preferred_element_type=F32)
            m_ref[hd] = m_new
        return carry

    lax.fori_loop(0, nk, step, 0)
    lane = lax.broadcasted_iota(jnp.int32, o_ref.shape, 1)
    out = jnp.where(lane < MLA_V, acc_ref[0] / l_ref[0], acc_ref[1] / l_ref[1])
    o_ref[...] = out.astype(o_ref.dtype)


def _attention(q, k, v, q_row0, q_rows, k_row0, k_rows, tq, tk):
    assert q_row0 % tq == 0 and q_rows % tq == 0 and k_row0 % k_rows == 0 and k_rows % tk == 0
    q0 = q_row0 // tq
    k0 = k_row0 // k_rows
    pair_w = MLA_PAIR * MLA_QK_PAD
    return pl.pallas_call(
        functools.partial(_attn_body, tk=tk, nk=k_rows // tk),
        grid=(MLA_HEADS // MLA_PAIR, q_rows // tq),
        in_specs=[pl.BlockSpec((tq, pair_w), lambda p, i: (i + q0, p)),
                  pl.BlockSpec((k_rows, pair_w), lambda p, i: (k0, p)),
                  pl.BlockSpec((k_rows, LANES), lambda p, i: (k0, p))],
        out_specs=pl.BlockSpec((tq, LANES), lambda p, i: (i, p)),
        out_shape=jax.ShapeDtypeStruct((q_rows, MLA_HEADS * MLA_V), BF16),
        scratch_shapes=[pltpu.VMEM((MLA_PAIR, tq, 1), F32), pltpu.VMEM((MLA_PAIR, tq, 1), F32),
                        pltpu.VMEM((MLA_PAIR, tq, LANES), F32)],
        compiler_params=_cparams(("arbitrary", "arbitrary")),
        name="mla_attention",
    )(q, k, v)


def _largest_divisor(n, candidates):
    for c in candidates:
        if n % c == 0:
            return c
    raise ValueError(f"no tile in {candidates} divides {n}")


def _pool_body(s_ref, sp_ref, sn_ref, mod_ref, gpre_ref, gpost_ref, w_ref, b_ref, sc_ref, o_ref):
    i = pl.program_id(0)
    nt = pl.num_programs(0)
    nct = CTX_LEN // ROW_TILE
    first, last = _stream_edges()
    s = s_ref[...]
    h = _pre(s, mod_ref, 1, gpre_ref)
    hp = jnp.where(first, 0.0, _pre(sp_ref[...], mod_ref, 1, gpre_ref))
    hn = jnp.where(last, 0.0, _pre(sn_ref[...], mod_ref, 1, gpre_ref))
    ext = jnp.concatenate([hp, h, hn], axis=0)
    er = ROW_TILE + 2 * POOL_HALO
    in_ctx = i >= nt - nct
    t0 = jnp.where(in_ctx, i - (nt - nct), i) * ROW_TILE
    t_len = jnp.where(in_ctx, nct, nt - nct) * ROW_TILE
    t = t0 + lax.broadcasted_iota(jnp.int32, (ROW_TILE, 1), 0)

    outs = []
    run = ext
    width = 1
    for gi, win in enumerate(POOL_WINDOWS):
        while width < win:
            run = run + pltpu.roll(run, width, 0)
            width *= 2
        ahead = win // 2 - 1
        grp = run[:, :POOL_GROUP]
        if ahead:
            grp = pltpu.roll(grp, er - ahead, 0)
        wsum = grp[POOL_HALO:POOL_HALO + ROW_TILE]
        cnt = (jnp.minimum(t + win // 2, t_len) - jnp.maximum(t - win // 2, 0)).astype(F32)
        diff = wsum / cnt - h[:, gi * POOL_GROUP:(gi + 1) * POOL_GROUP]
        outs.append(_bdot(diff, w_ref[gi]) + b_ref[gi:gi + 1, :])
        run = run[:, POOL_GROUP:]
    y = jnp.concatenate(outs, axis=-1) * sc_ref[...]
    o_ref[...] = _post(s, y, mod_ref, 1, gpost_ref, 1.0)


def _pool(s, mod, gpre, gpost, w, b, scale):
    n = s.shape[0]
    prev_spec, next_spec = _halo_specs(n)
    sc = scale.reshape(1, D_MODEL)
    wb = w.astype(BF16)
    return pl.pallas_call(
        _pool_body,
        grid=(n // ROW_TILE,),
        in_specs=[_row_spec(), prev_spec, next_spec, _mod_spec(n), _full(gpre), _full(gpost),
                  _full(wb), _full(b), _full(sc)],
        out_specs=_row_spec(),
        out_shape=jax.ShapeDtypeStruct(s.shape, F32),
        compiler_params=_cparams(("arbitrary",)),
        name="pool_mixer",
    )(s, s, s, mod, gpre, gpost, wb, b, sc)


def kernel(x, c, ctx, c_ctx, mod_w, mod_b, norm_pre, norm_post, ffn_w_gate, ffn_w_up, ffn_w_down,
           rw_mu, rw_w_r, rw_w_k, rw_w_v, rw_w_o, rw_w0, rw_w1, rw_w2, rw_a0, rw_a1, rw_a2,
           rw_v0, rw_v1, rw_v2, rw_g1, rw_g2, rw_k_k, rw_k_a, rw_r_k, rw_ln_w, rw_ln_b,
           mla_w_dq, mla_q_norm, mla_w_uq, mla_w_dkv, mla_kv_norm, mla_w_ukv, mla_w_o,
           pool_w, pool_b, pool_scale):
    batch, t, d = x.shape
    assert batch == 1 and d == D_MODEL and ctx.shape == (1, CTX_LEN, D_MODEL)
    assert t % ROW_TILE == 0 and CTX_LEN % ROW_TILE == 0 and t % GRID_W == 0
    n = t + CTX_LEN
    s = jnp.concatenate([x[0], ctx[0]], axis=0)
    c2 = jnp.concatenate([c_ctx[None], c, jnp.zeros((SUBLANES - 2, D_MODEL), F32)], axis=0)
    mod_all = _modulation(c2, mod_w, mod_b)
    v_first = None
    for i in range(DEPTH):
        kind, j = i % 3, i // 3
        mod = mod_all[i]
        gpre, gpost = norm_pre[i], norm_post[i]
        s = _ffn(s, mod, gpre, gpost, ffn_w_gate[i, 0].astype(BF16), ffn_w_up[i, 0].astype(BF16),
                 ffn_w_down[i, 0].astype(BF16), 0)
        if kind == 0:
            p = {'mu': rw_mu[j], 'w_r': rw_w_r[j], 'w_k': rw_w_k[j], 'w_v': rw_w_v[j],
                 'w0': rw_w0[j], 'w1': rw_w1[j], 'w2': rw_w2[j],
                 'a0': rw_a0[j], 'a1': rw_a1[j], 'a2': rw_a2[j],
                 'g1': rw_g1[j], 'g2': rw_g2[j], 'k_k': rw_k_k[j], 'k_a': rw_k_a[j],
                 'r_k': rw_r_k[j], 'ln_w': rw_ln_w[j], 'ln_b': rw_ln_b[j]}
            if j > 0:
                p['v0'], p['v1'], p['v2'] = rw_v0[j - 1], rw_v1[j - 1], rw_v2[j - 1]
            r, k, v, g, logw, a = _rwkv_proj(s, mod, gpre, p, v_first if j > 0 else None)
            if j == 0:
                v_first = v
            o = _wkv(logw, a, k, v, r, p)
            s = _mixer_out(o, g, rw_w_o[j].astype(BF16), s, mod, gpost)
        elif kind == 1:
            q, k, v = _mla_proj(s, mod, gpre, mla_w_dq[j], mla_q_norm[j], mla_w_uq[j],
                                mla_w_dkv[j], mla_kv_norm[j], mla_w_ukv[j])
            tq = _largest_divisor(t, (512, 256))
            tk = _largest_divisor(n, (1280, 256))
            o_x = _attention(q, k, v, 0, t, 0, n, tq, tk)
            o_c = _attention(q, k, v, t, CTX_LEN, t, CTX_LEN, CTX_LEN, CTX_LEN)
            o = jnp.concatenate([o_x, o_c], axis=0)
            s = _mixer_out(o, None, mla_w_o[j].astype(BF16), s, mod, gpost)
        else:
            s = _pool(s, mod, gpre, gpost, pool_w[j], pool_b[j], pool_scale[j])
        s = _ffn(s, mod, gpre, gpost, ffn_w_gate[i, 1].astype(BF16), ffn_w_up[i, 1].astype(BF16),
                 ffn_w_down[i, 1].astype(BF16), 2)
    return s[:t][None]
```

```python
import functools
import math

import jax
import jax.numpy as jnp
from jax import lax
from jax.experimental import pallas as pl
from jax.experimental.pallas import tpu as pltpu

F32 = jnp.float32
BF16 = jnp.bfloat16
HIGHEST = lax.Precision.HIGHEST

D_MODEL = 1024
DEPTH = 4
CTX_LEN = 256
GRID_W = 64
N_MOD = 9
FFN_RES = 0.5
NORM_EPS = 1e-6

RW_HEAD = 64
RW_GN_EPS = 64e-5
RW_GATE_LORA_PAD = 256
LORA_PAD = 128

MLA_HEADS = 16
MLA_NOPE = 64
MLA_ROPE = 32
MLA_V = 64
MLA_Q_RANK = 384
MLA_KV_RANK = 256
ROPE_BASE = 10000.0
MLA_QK_PAD = 128

POOL_WINDOWS = (2, 4, 8, 16)
POOL_GROUP = D_MODEL // len(POOL_WINDOWS)
POOL_HALO = 8

LANES = 128
SUBLANES = 8
ROW_TILE = 256
WKV_CHUNK = 64
VMEM_LIMIT = 56 * 1024 * 1024


def _cparams(sem):
    return pltpu.CompilerParams(dimension_semantics=sem, vmem_limit_bytes=VMEM_LIMIT)


def _full(a):
    nd = a.ndim
    return pl.BlockSpec(a.shape, lambda *_: (0,) * nd)


def _sigmoid(x):
    return 1.0 / (1.0 + jnp.exp(-x))


def _rms(x, g):
    return x * lax.rsqrt(jnp.mean(x * x, axis=-1, keepdims=True) + NORM_EPS) * g


def _pre(s, mod_ref, slot, gpre_ref):
    shift = mod_ref[0, 3 * slot:3 * slot + 1, :]
    scale = mod_ref[0, 3 * slot + 1:3 * slot + 2, :]
    return _rms(s, gpre_ref[slot:slot + 1, :]) * (1.0 + scale) + shift


def _post(s, y, mod_ref, slot, gpost_ref, weight):
    gate = mod_ref[0, 3 * slot + 2:3 * slot + 3, :]
    return s + (weight * gate) * _rms(y, gpost_ref[slot:slot + 1, :])


def _bdot(a, b):
    return jnp.dot(a.astype(BF16), b.astype(BF16), preferred_element_type=F32)


def _row_spec(width=D_MODEL):
    return pl.BlockSpec((ROW_TILE, width), lambda i: (i, 0))


def _mod_spec(n_rows):
    nxt = (n_rows - CTX_LEN) // ROW_TILE
    return pl.BlockSpec((1, N_MOD, D_MODEL), lambda i: (jnp.where(i >= nxt, 0, 1), 0, 0))


def _halo_specs(n_rows):
    per = ROW_TILE // SUBLANES
    nblk = n_rows // SUBLANES
    prev = pl.BlockSpec((SUBLANES, D_MODEL), lambda i: (jnp.maximum(i * per - 1, 0), 0))
    nxt = pl.BlockSpec((SUBLANES, D_MODEL), lambda i: (jnp.minimum((i + 1) * per, nblk - 1), 0))
    return prev, nxt


def _stream_edges():
    i = pl.program_id(0)
    nt = pl.num_programs(0)
    nxt = nt - CTX_LEN // ROW_TILE
    first = jnp.logical_or(i == 0, i == nxt)
    last = jnp.logical_or(i == nxt - 1, i == nt - 1)
    return first, last


def _mod_body(c_ref, w_ref, b_ref, o_ref):
    c = c_ref[...]
    s = c * _sigmoid(c)
    o_ref[0] = _bdot(s, w_ref[0]) + b_ref[0]


def _modulation(c2, mod_w, mod_b):
    tn = D_MODEL
    out = pl.pallas_call(
        _mod_body,
        grid=(DEPTH, N_MOD * D_MODEL // tn),
        in_specs=[pl.BlockSpec((SUBLANES, D_MODEL), lambda l, j: (0, 0)),
                  pl.BlockSpec((1, D_MODEL, tn), lambda l, j: (l, 0, j)),
                  pl.BlockSpec((1, 1, tn), lambda l, j: (l, 0, j))],
        out_specs=pl.BlockSpec((1, SUBLANES, tn), lambda l, j: (l, 0, j)),
        out_shape=jax.ShapeDtypeStruct((DEPTH, SUBLANES, N_MOD * D_MODEL), F32),
        compiler_params=_cparams(("arbitrary", "arbitrary")),
        name="modulation",
    )(c2, mod_w, mod_b.reshape(DEPTH, 1, N_MOD * D_MODEL))
    return out[:, :2].reshape(DEPTH, 2, N_MOD, D_MODEL)


def _ffn_body(s_ref, mod_ref, gpre_ref, gpost_ref, wg_ref, wu_ref, wd_ref, o_ref, *, slot):
    s = s_ref[...]
    h = _pre(s, mod_ref, slot, gpre_ref).astype(BF16)
    g = jnp.dot(h, wg_ref[...], preferred_element_type=F32)
    u = jnp.dot(h, wu_ref[...], preferred_element_type=F32)
    act = (g * _sigmoid(g)) * u
    y = jnp.dot(act.astype(BF16), wd_ref[...], preferred_element_type=F32)
    o_ref[...] = _post(s, y, mod_ref, slot, gpost_ref, FFN_RES)


def _ffn(s, mod, gpre, gpost, wg, wu, wd, slot):
    n = s.shape[0]
    return pl.pallas_call(
        functools.partial(_ffn_body, slot=slot),
        grid=(n // ROW_TILE,),
        in_specs=[_row_spec(), _mod_spec(n), _full(gpre), _full(gpost),
                  _full(wg), _full(wu), _full(wd)],
        out_specs=_row_spec(),
        out_shape=jax.ShapeDtypeStruct(s.shape, F32),
        compiler_params=_cparams(("arbitrary",)),
        name="ffn",
    )(s, mod, gpre, gpost, wg, wu, wd)


def _rwkv_proj_body(*refs, vres):
    (s_ref, sp_ref, sn_ref, mod_ref, gpre_ref, mu_ref, wr_ref, wk_ref, wv_ref,
     w0_ref, w1_ref, w2_ref, a0_ref, a1_ref, a2_ref, g1_ref, g2_ref) = refs[:17]
    refs = refs[17:]
    if vres:
        v0_ref, v1_ref, v2_ref, vf_ref = refs[:4]
        refs = refs[4:]
    r_ref, k_ref, v_ref, g_ref, lw_ref, a_ref = refs

    first, last = _stream_edges()
    h = _pre(s_ref[...], mod_ref, 1, gpre_ref)
    hp = _pre(sp_ref[SUBLANES - 1:SUBLANES, :], mod_ref, 1, gpre_ref)
    hn = _pre(sn_ref[0:1, :], mod_ref, 1, gpre_ref)
    hp = jnp.where(first, 0.0, hp)
    hn = jnp.where(last, 0.0, hn)
    row = lax.broadcasted_iota(jnp.int32, h.shape, 0)
    prev = jnp.where(row == 0, hp, pltpu.roll(h, 1, 0))
    nxt = jnp.where(row == ROW_TILE - 1, hn, pltpu.roll(h, ROW_TILE - 1, 0))
    xx = 0.5 * (prev + nxt) - h

    def mix(n):
        return h + xx * mu_ref[n:n + 1, :]

    r_ref[...] = _bdot(mix(0), wr_ref[...])
    k_ref[...] = _bdot(mix(2), wk_ref[...])
    xvb = mix(3).astype(BF16)
    v = jnp.dot(xvb, wv_ref[...], preferred_element_type=F32)
    if vres:
        gate = _sigmoid(v0_ref[...] + _bdot(jnp.dot(xvb, v1_ref[...], preferred_element_type=F32),
                                            v2_ref[...]))
        v = v + (vf_ref[...] - v) * gate
    v_ref[...] = v
    g_ref[...] = _bdot(_sigmoid(_bdot(mix(5), g1_ref[...])), g2_ref[...])
    lw = w0_ref[...] + _bdot(jnp.tanh(_bdot(mix(1), w1_ref[...])), w2_ref[...])
    logw = -math.exp(-0.5) * _sigmoid(lw)
    aa = _sigmoid(a0_ref[...] + _bdot(_bdot(mix(4), a1_ref[...]), a2_ref[...]))
    for d in range(2):
        lw_ref[d] = logw[:, d * D_MODEL:(d + 1) * D_MODEL]
        a_ref[d] = aa[:, d * D_MODEL:(d + 1) * D_MODEL]


def _lora_pair(w1, w2, w0):
    lora = w1.shape[-1]
    w1c = jnp.concatenate([w1[0], w1[1]], axis=1)
    z = jnp.zeros((lora, D_MODEL), w2.dtype)
    w2c = jnp.concatenate([jnp.concatenate([w2[0], z], axis=1),
                           jnp.concatenate([z, w2[1]], axis=1)], axis=0)
    return w1c.astype(BF16), w2c.astype(BF16), w0.reshape(1, 2 * D_MODEL)


def _pad_to(a, axis, size):
    pad = [(0, 0)] * a.ndim
    pad[axis] = (0, size - a.shape[axis])
    return jnp.pad(a, pad)


def _rwkv_proj(s, mod, gpre, p, v_first):
    n = s.shape[0]
    vres = v_first is not None
    w1c, w2c, w0c = _lora_pair(p['w1'], p['w2'], p['w0'])
    a1c, a2c, a0c = _lora_pair(p['a1'], p['a2'], p['a0'])
    g1 = _pad_to(p['g1'], 1, RW_GATE_LORA_PAD).astype(BF16)
    g2 = _pad_to(p['g2'], 0, RW_GATE_LORA_PAD).astype(BF16)
    prev_spec, next_spec = _halo_specs(n)
    args = [s, s, s, mod, gpre, p['mu'], p['w_r'].astype(BF16), p['w_k'].astype(BF16),
            p['w_v'].astype(BF16), w0c, w1c, w2c, a0c, a1c, a2c, g1, g2]
    specs = [_row_spec(), prev_spec, next_spec, _mod_spec(n)] + [_full(a) for a in args[4:]]
    if vres:
        extra = [p['v0'].reshape(1, D_MODEL), _pad_to(p['v1'], 1, LORA_PAD).astype(BF16),
                 _pad_to(p['v2'], 0, LORA_PAD).astype(BF16)]
        args += extra + [v_first]
        specs += [_full(a) for a in extra] + [_row_spec()]
    row_out = jax.ShapeDtypeStruct((n, D_MODEL), F32)
    dir_out = jax.ShapeDtypeStruct((2, n, D_MODEL), F32)
    dir_spec = pl.BlockSpec((2, ROW_TILE, D_MODEL), lambda i: (0, i, 0))
    return pl.pallas_call(
        functools.partial(_rwkv_proj_body, vres=vres),
        grid=(n // ROW_TILE,),
        in_specs=specs,
        out_specs=[_row_spec()] * 4 + [dir_spec] * 2,
        out_shape=[row_out] * 4 + [dir_out] * 2,
        compiler_params=_cparams(("arbitrary",)),
        name="rwkv_proj",
    )(*args)


WKV_PAIR = LANES // RW_HEAD
WKV_ROWS = WKV_PAIR * WKV_CHUNK
WKV_GROUPS = D_MODEL // LANES
WKV_INV_LEVELS = tuple(range(1, WKV_CHUNK.bit_length() - 1))


def _wkv_masks(sgn):
    c = WKV_CHUNK
    ti = lax.broadcasted_iota(jnp.int32, (c, c), 0)
    si = lax.broadcasted_iota(jnp.int32, (c, c), 1)
    incl_c = ((ti - si) * sgn >= 0).astype(F32)
    rows = lax.broadcasted_iota(jnp.int32, (WKV_ROWS, LANES), 0)
    lanes = lax.broadcasted_iota(jnp.int32, (WKV_ROWS, LANES), 1)
    keep = (rows < c) == (lanes < RW_HEAD)
    tr = lax.broadcasted_iota(jnp.int32, (WKV_ROWS, WKV_ROWS), 0)
    sr = lax.broadcasted_iota(jnp.int32, (WKV_ROWS, WKV_ROWS), 1)
    same = (tr < c) == (sr < c)
    strict = jnp.logical_and(same, (tr - sr) * sgn > 0)
    incl = jnp.logical_and(same, (tr - sr) * sgn >= 0)
    eye = (tr == sr).astype(F32)
    blk = lambda x, b: lax.shift_right_logical(x, b)
    pairs = [blk(tr, 1) == blk(sr, 1)]
    for b in WKV_INV_LEVELS:
        pairs.append(jnp.logical_and(blk(tr, b + 1) == blk(sr, b + 1), blk(tr, b) != blk(sr, b)))
    return incl_c, keep, strict, incl, eye, pairs


def _dot_nt(x, y):
    return lax.dot_general(x.astype(BF16), y.astype(BF16), (((1,), (1,)), ((), ())),
                           preferred_element_type=F32)


def _pdot(x, y, prec):
    if prec is None:
        return _bdot(x, y)
    return jnp.dot(x, y, precision=prec, preferred_element_type=F32)


def _wkv_group(lw, cum, a, k, v, r, k_k, k_a, r_k, ln_w, ln_b, st_ref, o_ref, masks,
               prec_inv, prec_state):
    c = WKV_CHUNK
    _, keep, strict, incl, eye, pairs = masks
    head_a = lax.broadcasted_iota(jnp.int32, (c, LANES), 1) < RW_HEAD

    def head_sum(x):
        sa = jnp.sum(jnp.where(head_a, x, 0.0), axis=-1, keepdims=True)
        sb = jnp.sum(jnp.where(head_a, 0.0, x), axis=-1, keepdims=True)
        return jnp.where(head_a, sa, sb)

    kkf = k * k_k
    kap = kkf * lax.rsqrt(jnp.maximum(head_sum(kkf * kkf), 1e-24))
    kd = k * (1.0 + (a - 1.0) * k_a)
    b = kap * a

    tot = jnp.sum(lw, axis=0, keepdims=True)
    e_neg = jnp.exp(-cum)
    e_rem = jnp.exp(tot - cum)
    kt = kap * jnp.exp(cum - lw)
    rt = r * jnp.exp(cum)
    g_c = jnp.exp(tot)

    def stack(x):
        return jnp.where(keep, jnp.concatenate([x, x], axis=0), 0.0)

    kt_s, rt_s, v_s = stack(kt), stack(rt), stack(v)
    kh_s, bh_s = stack(kd * e_neg), stack(b * e_neg)
    kb_s, bb_s = stack(kd * e_rem), stack(b * e_rem)
    kbt, bbt = kb_s.T.astype(BF16), bb_s.T.astype(BF16)
    bonus = head_sum(r * kd * r_k) * v
    yield

    r2 = WKV_ROWS
    prod = _dot_nt(jnp.concatenate([kt_s, rt_s], axis=0), jnp.concatenate([kh_s, bh_s], axis=0))
    l_k = jnp.where(strict, prod[:r2, :r2], 0.0)
    l_b = jnp.where(strict, prod[:r2, r2:], 0.0)
    p_k = jnp.where(incl, prod[r2:, :r2], 0.0).astype(BF16)
    p_b = jnp.where(incl, prod[r2:, r2:], 0.0).astype(BF16)
    yield

    lkv = _bdot(l_k, v_s)
    t_inv = eye - jnp.where(pairs[0], l_b, 0.0)
    for lvl in range(1, len(pairs)):
        x = _pdot(jnp.where(pairs[lvl], l_b, 0.0), t_inv, prec_inv)
        yield
        t_inv = t_inv - _pdot(t_inv, x, prec_inv)
        yield

    wu = _bdot(t_inv, jnp.concatenate([kt_s, lkv], axis=1))
    yield
    wu = wu.astype(BF16)
    pbwu = jnp.dot(p_b, wu, preferred_element_type=F32)
    q_m = rt_s - pbwu[:, :LANES]
    y_0 = _bdot(p_k, v_s) - pbwu[:, LANES:]
    yield
    bbwu = jnp.dot(bbt, wu, preferred_element_type=F32)
    g_m = eye * g_c - bbwu[:, :LANES]
    h_m = _bdot(kbt, v_s) - bbwu[:, LANES:]
    yield

    a_state = st_ref[...]
    y_s = _pdot(q_m, a_state, prec_state) + y_0
    st_ref[...] = _pdot(g_m, a_state, prec_state) + h_m
    y = y_s[:c] + y_s[c:]

    mu = head_sum(y) * (1.0 / RW_HEAD)
    yc = y - mu
    var = head_sum(yc * yc) * (1.0 / RW_HEAD)
    yn = yc * lax.rsqrt(var + RW_GN_EPS) * ln_w + ln_b
    o_ref[...] = yn + bonus


def _wkv_body(lw_ref, a_ref, k_ref, v_ref, r_ref, kk_ref, ka_ref, rk_ref, lnw_ref, lnb_ref,
              o_ref, st_ref, *, prec_inv, prec_state):
    @pl.when(pl.program_id(1) == 0)
    def _():
        st_ref[...] = jnp.zeros_like(st_ref)

    masks = _wkv_masks(1 - 2 * pl.program_id(0))
    cum = jnp.dot(masks[0], lw_ref[0], precision=HIGHEST, preferred_element_type=F32)
    groups = []
    for g in range(WKV_GROUPS):
        sl = slice(g * LANES, (g + 1) * LANES)
        groups.append(_wkv_group(lw_ref[0, :, sl], cum[:, sl], a_ref[0, :, sl], k_ref[:, sl],
                                 v_ref[:, sl], r_ref[:, sl], kk_ref[:, sl], ka_ref[:, sl],
                                 rk_ref[:, sl], lnw_ref[:, sl], lnb_ref[:, sl], st_ref.at[g],
                                 o_ref.at[0, :, sl], masks, prec_inv, prec_state))
    while groups:
        groups = [gen for gen in groups if next(gen, "done") != "done"]


def _wkv(logw, a, k, v, r, p, prec_inv=None, prec_state=None):
    n = k.shape[0]
    nc = n // WKV_CHUNK
    ncc = CTX_LEN // WKV_CHUNK
    ncx = nc - ncc

    def chunk(d, j):
        fwd = jnp.where(j < ncc, ncx + j, j - ncc)
        rev = jnp.where(j < ncc, nc - 1 - j, nc - 1 - j)
        return jnp.where(d == 0, fwd, rev)

    dir_spec = pl.BlockSpec((1, WKV_CHUNK, D_MODEL), lambda d, j: (d, chunk(d, j), 0))
    row_spec = pl.BlockSpec((WKV_CHUNK, D_MODEL), lambda d, j: (chunk(d, j), 0))
    vecs = [p['k_k'].reshape(1, D_MODEL), p['k_a'].reshape(1, D_MODEL),
            p['r_k'].reshape(1, D_MODEL), p['ln_w'].reshape(1, D_MODEL),
            p['ln_b'].reshape(1, D_MODEL)]
    return pl.pallas_call(
        functools.partial(_wkv_body, prec_inv=prec_inv, prec_state=prec_state),
        grid=(2, nc),
        in_specs=[dir_spec, dir_spec, row_spec, row_spec, row_spec] + [_full(x) for x in vecs],
        out_specs=dir_spec,
        out_shape=jax.ShapeDtypeStruct((2, n, D_MODEL), F32),
        scratch_shapes=[pltpu.VMEM((WKV_GROUPS, WKV_ROWS, LANES), F32)],
        compiler_params=_cparams(("arbitrary", "arbitrary")),
        name="wkv_scan",
    )(logw, a, k, v, r, *vecs)


def _out_body(*refs, gated):
    if gated:
        o_ref, g_ref, w_ref, s_ref, mod_ref, gpost_ref, out_ref = refs
        y = (o_ref[0] + o_ref[1]) * g_ref[...]
    else:
        o_ref, w_ref, s_ref, mod_ref, gpost_ref, out_ref = refs
        y = o_ref[...]
    y = jnp.dot(y.astype(BF16), w_ref[...], preferred_element_type=F32)
    out_ref[...] = _post(s_ref[...], y, mod_ref, 1, gpost_ref, 1.0)


def _mixer_out(o, g, w_o, s, mod, gpost):
    n = s.shape[0]
    gated = g is not None
    if gated:
        args = [o, g, w_o, s, mod, gpost]
        specs = [pl.BlockSpec((2, ROW_TILE, D_MODEL), lambda i: (0, i, 0)), _row_spec(),
                 _full(w_o), _row_spec(), _mod_spec(n), _full(gpost)]
    else:
        args = [o, w_o, s, mod, gpost]
        specs = [_row_spec(), _full(w_o), _row_spec(), _mod_spec(n), _full(gpost)]
    return pl.pallas_call(
        functools.partial(_out_body, gated=gated),
        grid=(n // ROW_TILE,),
        in_specs=specs,
        out_specs=_row_spec(),
        out_shape=jax.ShapeDtypeStruct(s.shape, F32),
        compiler_params=_cparams(("arbitrary",)),
        name="mixer_out",
    )(*args)


def _mla_proj_body(s_ref, mod_ref, gpre_ref, wdq_ref, qn_ref, wq1_ref, wq2_ref, wdkv_ref, kvn_ref,
                   wkr_ref, wk_ref, wv_ref, place_ref, tq_ref, tk_ref, q_ref, k_ref, v_ref):
    h = _pre(s_ref[...], mod_ref, 1, gpre_ref).astype(BF16)
    cq = jnp.dot(h, wdq_ref[...], preferred_element_type=F32)
    qn = _rms(cq, qn_ref[...]).astype(BF16)
    q1 = jnp.dot(qn, wq1_ref[...], preferred_element_type=F32)
    q2 = jnp.dot(qn, wq2_ref[...], preferred_element_type=F32)
    ckv = jnp.dot(h, wdkv_ref[...], preferred_element_type=F32)
    kvn = _rms(ckv, kvn_ref[...]).astype(BF16)
    kr = jnp.dot(h, wkr_ref[...], preferred_element_type=F32)
    krr = kr * tk_ref[0] + pltpu.roll(kr, LANES - MLA_ROPE, 1) * tk_ref[1]
    kk = (jnp.dot(kvn, wk_ref[...], preferred_element_type=F32)
          + jnp.dot(krr.astype(BF16), place_ref[...], preferred_element_type=F32))
    k_ref[...] = kk.astype(BF16)
    v_ref[...] = jnp.dot(kvn, wv_ref[...], preferred_element_type=F32).astype(BF16)
    cos = tq_ref[0]
    sin = tq_ref[1]
    for hd in range(MLA_HEADS):
        sl = slice(hd * MLA_QK_PAD, (hd + 1) * MLA_QK_PAD)
        q_ref[:, sl] = (q1[:, sl] * cos + q2[:, sl] * sin).astype(BF16)


def _rot_cols(w):
    half = MLA_ROPE // 4
    parts = []
    for ax in range(2):
        blk = w[..., ax * 2 * half:(ax + 1) * 2 * half]
        parts += [-blk[..., half:], blk[..., :half]]
    return jnp.concatenate(parts, axis=-1)


def _mla_tables(n):
    t = n - CTX_LEN
    pos = jnp.arange(t, dtype=jnp.int32)
    rowp = (pos // GRID_W).astype(F32)
    colp = (pos % GRID_W).astype(F32)
    axis_dim = MLA_ROPE // 2
    inv = ROPE_BASE ** (-jnp.arange(0, axis_dim, 2, dtype=F32) / axis_dim)
    ar = rowp[:, None] * inv
    ac = colp[:, None] * inv
    cosf = jnp.concatenate([jnp.cos(ar), jnp.cos(ar), jnp.cos(ac), jnp.cos(ac)], axis=-1)
    sinf = jnp.concatenate([jnp.sin(ar), jnp.sin(ar), jnp.sin(ac), jnp.sin(ac)], axis=-1)
    cosf = jnp.concatenate([cosf, jnp.ones((CTX_LEN, MLA_ROPE), F32)], axis=0)
    sinf = jnp.concatenate([sinf, jnp.zeros((CTX_LEN, MLA_ROPE), F32)], axis=0)
    scale = (MLA_NOPE + MLA_ROPE) ** -0.5
    zq = jnp.zeros((n, MLA_QK_PAD - MLA_NOPE - MLA_ROPE), F32)
    tq = jnp.stack([jnp.concatenate([jnp.ones((n, MLA_NOPE), F32), cosf, zq], axis=-1),
                    jnp.concatenate([jnp.zeros((n, MLA_NOPE), F32), sinf, zq], axis=-1)]) * scale
    zk = jnp.zeros((n, LANES - MLA_ROPE), F32)
    tk = jnp.stack([jnp.concatenate([cosf, zk], axis=-1), jnp.concatenate([sinf, zk], axis=-1)])
    return tq, tk


def _mla_proj(s, mod, gpre, w_dq, q_norm, w_uq, w_dkv, kv_norm, w_ukv):
    n = s.shape[0]
    hq = MLA_NOPE + MLA_ROPE
    wq = w_uq.reshape(MLA_Q_RANK, MLA_HEADS, hq)
    zq = jnp.zeros((MLA_Q_RANK, MLA_HEADS, MLA_QK_PAD - hq), w_uq.dtype)
    wq1 = jnp.concatenate([wq, zq], axis=-1).reshape(MLA_Q_RANK, MLA_HEADS * MLA_QK_PAD)
    wq2 = jnp.concatenate([jnp.zeros_like(wq[..., :MLA_NOPE]), _rot_cols(wq[..., MLA_NOPE:]), zq],
                          axis=-1).reshape(MLA_Q_RANK, MLA_HEADS * MLA_QK_PAD)
    wkv = w_ukv.reshape(MLA_KV_RANK, MLA_HEADS, MLA_NOPE + MLA_V)
    wk = jnp.concatenate([wkv[..., :MLA_NOPE],
                          jnp.zeros((MLA_KV_RANK, MLA_HEADS, MLA_QK_PAD - MLA_NOPE), w_ukv.dtype)],
                         axis=-1).reshape(MLA_KV_RANK, MLA_HEADS * MLA_QK_PAD)
    wv = wkv[..., MLA_NOPE:].reshape(MLA_KV_RANK, MLA_HEADS * MLA_V)
    w_rope = w_dkv[:, MLA_KV_RANK:]
    wkr = jnp.concatenate([w_rope, _rot_cols(w_rope),
                           jnp.zeros((D_MODEL, LANES - 2 * MLA_ROPE), w_dkv.dtype)], axis=-1)
    src = jnp.arange(LANES)[:, None]
    dst = jnp.arange(MLA_HEADS * MLA_QK_PAD)[None, :]
    place = jnp.logical_and(src < MLA_ROPE, dst % MLA_QK_PAD == MLA_NOPE + src).astype(BF16)
    tq, tk = _mla_tables(n)
    weights = [w_dq.astype(BF16), q_norm.reshape(1, MLA_Q_RANK), wq1.astype(BF16), wq2.astype(BF16),
               w_dkv[:, :MLA_KV_RANK].astype(BF16), kv_norm.reshape(1, MLA_KV_RANK),
               wkr.astype(BF16), wk.astype(BF16), wv.astype(BF16), place]
    tab_spec = pl.BlockSpec((2, ROW_TILE, LANES), lambda i: (0, i, 0))
    return pl.pallas_call(
        _mla_proj_body,
        grid=(n // ROW_TILE,),
        in_specs=[_row_spec(), _mod_spec(n), _full(gpre)] + [_full(w) for w in weights]
                 + [tab_spec, tab_spec],
        out_specs=[_row_spec(MLA_HEADS * MLA_QK_PAD), _row_spec(MLA_HEADS * MLA_QK_PAD),
                   _row_spec(MLA_HEADS * MLA_V)],
        out_shape=[jax.ShapeDtypeStruct((n, MLA_HEADS * MLA_QK_PAD), BF16),
                   jax.ShapeDtypeStruct((n, MLA_HEADS * MLA_QK_PAD), BF16),
                   jax.ShapeDtypeStruct((n, MLA_HEADS * MLA_V), BF16)],
        compiler_params=_cparams(("arbitrary",)),
        name="mla_proj",
    )(s, mod, gpre, *weights, tq, tk)


MLA_PAIR = LANES // MLA_V


def _attn_body(q_ref, k_ref, v_ref, o_ref, m_ref, l_ref, acc_ref, *, tk, nk):
    m_ref[...] = jnp.full_like(m_ref, -jnp.inf)
    l_ref[...] = jnp.zeros_like(l_ref)
    acc_ref[...] = jnp.zeros_like(acc_ref)

    def step(i, carry):
        rows = pl.ds(pl.multiple_of(i * tk, tk), tk)
        vv = v_ref[rows, :]
        for hd in range(MLA_PAIR):
            sl = slice(hd * MLA_QK_PAD, (hd + 1) * MLA_QK_PAD)
            s = lax.dot_general(q_ref[:, sl], k_ref[rows, sl], (((1,), (1,)), ((), ())),
                                preferred_element_type=F32)
            m_old = m_ref[hd]
            m_new = jnp.maximum(m_old, jnp.max(s, axis=-1, keepdims=True))
            alpha = jnp.exp(m_old - m_new)
            p = jnp.exp(s - m_new)
            l_ref[hd] = alpha * l_ref[hd] + jnp.sum(p, axis=-1, keepdims=True)
            acc_ref[hd] = alpha * acc_ref[hd] + jnp.dot(p.astype(BF16), vv,
                                                        preferred_element_type=F32)
            m_ref[hd] = m_new
        return carry

    lax.fori_loop(0, nk, step, 0)
    lane = lax.broadcasted_iota(jnp.int32, o_ref.shape, 1)
    out = jnp.where(lane < MLA_V, acc_ref[0] / l_ref[0], acc_ref[1] / l_ref[1])
    o_ref[...] = out.astype(o_ref.dtype)


def _attention(q, k, v, q_row0, q_rows, k_row0, k_rows, tq, tk):
    assert q_row0 % tq == 0 and q_rows % tq == 0 and k_row0 % k_rows == 0 and k_rows % tk == 0
    q0 = q_row0 // tq
    k0 = k_row0 // k_rows
    pair_w = MLA_PAIR * MLA_QK_PAD
    return pl.pallas_call(
        functools.partial(_attn_body, tk=tk, nk=k_rows // tk),
        grid=(MLA_HEADS // MLA_PAIR, q_rows // tq),
        in_specs=[pl.BlockSpec((tq, pair_w), lambda p, i: (i + q0, p)),
                  pl.BlockSpec((k_rows, pair_w), lambda p, i: (k0, p)),
                  pl.BlockSpec((k_rows, LANES), lambda p, i: (k0, p))],
        out_specs=pl.BlockSpec((tq, LANES), lambda p, i: (i, p)),
        out_shape=jax.ShapeDtypeStruct((q_rows, MLA_HEADS * MLA_V), BF16),
        scratch_shapes=[pltpu.VMEM((MLA_PAIR, tq, 1), F32), pltpu.VMEM((MLA_PAIR, tq, 1), F32),
                        pltpu.VMEM((MLA_PAIR, tq, LANES), F32)],
        compiler_params=_cparams(("arbitrary", "arbitrary")),
        name="mla_attention",
    )(q, k, v)


def _largest_divisor(n, candidates):
    for c in candidates:
        if n % c == 0:
            return c
    raise ValueError(f"no tile in {candidates} divides {n}")


def _pool_body(s_ref, sp_ref, sn_ref, mod_ref, gpre_ref, gpost_ref, w_ref, b_ref, sc_ref, o_ref):
    i = pl.program_id(0)
    nt = pl.num_programs(0)
    nct = CTX_LEN // ROW_TILE
    first, last = _stream_edges()
    s = s_ref[...]
    h = _pre(s, mod_ref, 1, gpre_ref)
    hp = jnp.where(first, 0.0, _pre(sp_ref[...], mod_ref, 1, gpre_ref))
    hn = jnp.where(last, 0.0, _pre(sn_ref[...], mod_ref, 1, gpre_ref))
    ext = jnp.concatenate([hp, h, hn], axis=0)
    er = ROW_TILE + 2 * POOL_HALO
    in_ctx = i >= nt - nct
    t0 = jnp.where(in_ctx, i - (nt - nct), i) * ROW_TILE
    t_len = jnp.where(in_ctx, nct, nt - nct) * ROW_TILE
    t = t0 + lax.broadcasted_iota(jnp.int32, (ROW_TILE, 1), 0)

    outs = []
    run = ext
    width = 1
    for gi, win in enumerate(POOL_WINDOWS):
        while width < win:
            run = run + pltpu.roll(run, width, 0)
            width *= 2
        ahead = win // 2 - 1
        grp = run[:, :POOL_GROUP]
        if ahead:
            grp = pltpu.roll(grp, er - ahead, 0)
        wsum = grp[POOL_HALO:POOL_HALO + ROW_TILE]
        cnt = (jnp.minimum(t + win // 2, t_len) - jnp.maximum(t - win // 2, 0)).astype(F32)
        diff = wsum / cnt - h[:, gi * POOL_GROUP:(gi + 1) * POOL_GROUP]
        outs.append(_bdot(diff, w_ref[gi]) + b_ref[gi:gi + 1, :])
        run = run[:, POOL_GROUP:]
    y = jnp.concatenate(outs, axis=-1) * sc_ref[...]
    o_ref[...] = _post(s, y, mod_ref, 1, gpost_ref, 1.0)


def _pool(s, mod, gpre, gpost, w, b, scale):
    n = s.shape[0]
    prev_spec, next_spec = _halo_specs(n)
    sc = scale.reshape(1, D_MODEL)
    wb = w.astype(BF16)
    return pl.pallas_call(
        _pool_body,
        grid=(n // ROW_TILE,),
        in_specs=[_row_spec(), prev_spec, next_spec, _mod_spec(n), _full(gpre), _full(gpost),
                  _full(wb), _full(b), _full(sc)],
        out_specs=_row_spec(),
        out_shape=jax.ShapeDtypeStruct(s.shape, F32),
        compiler_params=_cparams(("arbitrary",)),
        name="pool_mixer",
    )(s, s, s, mod, gpre, gpost, wb, b, sc)


def kernel(x, c, ctx, c_ctx, mod_w, mod_b, norm_pre, norm_post, ffn_w_gate, ffn_w_up, ffn_w_down,
           rw_mu, rw_w_r, rw_w_k, rw_w_v, rw_w_o, rw_w0, rw_w1, rw_w2, rw_a0, rw_a1, rw_a2,
           rw_v0, rw_v1, rw_v2, rw_g1, rw_g2, rw_k_k, rw_k_a, rw_r_k, rw_ln_w, rw_ln_b,
           mla_w_dq, mla_q_norm, mla_w_uq, mla_w_dkv, mla_kv_norm, mla_w_ukv, mla_w_o,
           pool_w, pool_b, pool_scale):
    batch, t, d = x.shape
    assert batch == 1 and d == D_MODEL and ctx.shape == (1, CTX_LEN, D_MODEL)
    assert t % ROW_TILE == 0 and CTX_LEN % ROW_TILE == 0 and t % GRID_W == 0
    n = t + CTX_LEN
    s = jnp.concatenate([x[0], ctx[0]], axis=0)
    c2 = jnp.concatenate([c_ctx[None], c, jnp.zeros((SUBLANES - 2, D_MODEL), F32)], axis=0)
    mod_all = _modulation(c2, mod_w, mod_b)
    v_first = None
    for i in range(DEPTH):
        kind, j = i % 3, i // 3
        mod = mod_all[i]
        gpre, gpost = norm_pre[i], norm_post[i]
        s = _ffn(s, mod, gpre, gpost, ffn_w_gate[i, 0].astype(BF16), ffn_w_up[i, 0].astype(BF16),
                 ffn_w_down[i, 0].astype(BF16), 0)
        if kind == 0:
            p = {'mu': rw_mu[j], 'w_r': rw_w_r[j], 'w_k': rw_w_k[j], 'w_v': rw_w_v[j],
                 'w0': rw_w0[j], 'w1': rw_w1[j], 'w2': rw_w2[j],
                 'a0': rw_a0[j], 'a1': rw_a1[j], 'a2': rw_a2[j],
                 'g1': rw_g1[j], 'g2': rw_g2[j], 'k_k': rw_k_k[j], 'k_a': rw_k_a[j],
                 'r_k': rw_r_k[j], 'ln_w': rw_ln_w[j], 'ln_b': rw_ln_b[j]}
            if j > 0:
                p['v0'], p['v1'], p['v2'] = rw_v0[j - 1], rw_v1[j - 1], rw_v2[j - 1]
            r, k, v, g, logw, a = _rwkv_proj(s, mod, gpre, p, v_first if j > 0 else None)
            if j == 0:
                v_first = v
            o = _wkv(logw, a, k, v, r, p)
            s = _mixer_out(o, g, rw_w_o[j].astype(BF16), s, mod, gpost)
        elif kind == 1:
            q, k, v = _mla_proj(s, mod, gpre, mla_w_dq[j], mla_q_norm[j], mla_w_uq[j],
                                mla_w_dkv[j], mla_kv_norm[j], mla_w_ukv[j])
            tq = _largest_divisor(t, (512, 256))
            tk = _largest_divisor(n, (1280, 256))
            o_x = _attention(q, k, v, 0, t, 0, n, tq, tk)
            o_c = _attention(q, k, v, t, CTX_LEN, t, CTX_LEN, CTX_LEN, CTX_LEN)
            o = jnp.concatenate([o_x, o_c], axis=0)
            s = _mixer_out(o, None, mla_w_o[j].astype(BF16), s, mod, gpost)
        else:
            s = _pool(s, mod, gpre, gpost, pool_w[j], pool_b[j], pool_scale[j])
        s = _ffn(s, mod, gpre, gpost, ffn_w_gate[i, 1].astype(BF16), ffn_w_up[i, 1].astype(BF16),
                 ffn_w_down[i, 1].astype(BF16), 2)
    return s[:t][None]
```

```python
import functools
import math

import jax
import jax.numpy as jnp
from jax import lax
from jax.experimental import pallas as pl
from jax.experimental.pallas import tpu as pltpu

F32 = jnp.float32
BF16 = jnp.bfloat16
HIGHEST = lax.Precision.HIGHEST

D_MODEL = 1024
DEPTH = 4
CTX_LEN = 256
GRID_W = 64
N_MOD = 9
FFN_RES = 0.5
NORM_EPS = 1e-6

RW_HEAD = 64
RW_GN_EPS = 64e-5
RW_GATE_LORA_PAD = 256
LORA_PAD = 128

MLA_HEADS = 16
MLA_NOPE = 64
MLA_ROPE = 32
MLA_V = 64
MLA_Q_RANK = 384
MLA_KV_RANK = 256
ROPE_BASE = 10000.0
MLA_QK_PAD = 128

POOL_WINDOWS = (2, 4, 8, 16)
POOL_GROUP = D_MODEL // len(POOL_WINDOWS)
POOL_HALO = 8

LANES = 128
SUBLANES = 8
ROW_TILE = 256
WKV_CHUNK = 64
WKV_STEP_CHUNKS = 2
VMEM_LIMIT = 56 * 1024 * 1024


def _cparams(sem):
    return pltpu.CompilerParams(dimension_semantics=sem, vmem_limit_bytes=VMEM_LIMIT)


def _full(a):
    nd = a.ndim
    return pl.BlockSpec(a.shape, lambda *_: (0,) * nd)


def _sigmoid(x):
    return 1.0 / (1.0 + jnp.exp(-x))


def _rms(x, g):
    return x * lax.rsqrt(jnp.mean(x * x, axis=-1, keepdims=True) + NORM_EPS) * g


def _pre(s, mod_ref, slot, gpre_ref):
    shift = mod_ref[0, 3 * slot:3 * slot + 1, :]
    scale = mod_ref[0, 3 * slot + 1:3 * slot + 2, :]
    return _rms(s, gpre_ref[slot:slot + 1, :]) * (1.0 + scale) + shift


def _post(s, y, mod_ref, slot, gpost_ref, weight):
    gate = mod_ref[0, 3 * slot + 2:3 * slot + 3, :]
    return s + (weight * gate) * _rms(y, gpost_ref[slot:slot + 1, :])


def _bdot(a, b):
    return jnp.dot(a.astype(BF16), b.astype(BF16), preferred_element_type=F32)


def _row_spec(width=D_MODEL):
    return pl.BlockSpec((ROW_TILE, width), lambda i: (i, 0))


def _mod_spec(n_rows):
    nxt = (n_rows - CTX_LEN) // ROW_TILE
    return pl.BlockSpec((1, N_MOD, D_MODEL), lambda i: (jnp.where(i >= nxt, 0, 1), 0, 0))


def _halo_specs(n_rows):
    per = ROW_TILE // SUBLANES
    nblk = n_rows // SUBLANES
    prev = pl.BlockSpec((SUBLANES, D_MODEL), lambda i: (jnp.maximum(i * per - 1, 0), 0))
    nxt = pl.BlockSpec((SUBLANES, D_MODEL), lambda i: (jnp.minimum((i + 1) * per, nblk - 1), 0))
    return prev, nxt


def _stream_edges():
    i = pl.program_id(0)
    nt = pl.num_programs(0)
    nxt = nt - CTX_LEN // ROW_TILE
    first = jnp.logical_or(i == 0, i == nxt)
    last = jnp.logical_or(i == nxt - 1, i == nt - 1)
    return first, last


def _mod_body(c_ref, w_ref, b_ref, o_ref):
    c = c_ref[...]
    s = c * _sigmoid(c)
    o_ref[0] = _bdot(s, w_ref[0]) + b_ref[0]


def _modulation(c2, mod_w, mod_b):
    tn = D_MODEL
    out = pl.pallas_call(
        _mod_body,
        grid=(DEPTH, N_MOD * D_MODEL // tn),
        in_specs=[pl.BlockSpec((SUBLANES, D_MODEL), lambda l, j: (0, 0)),
                  pl.BlockSpec((1, D_MODEL, tn), lambda l, j: (l, 0, j)),
                  pl.BlockSpec((1, 1, tn), lambda l, j: (l, 0, j))],
        out_specs=pl.BlockSpec((1, SUBLANES, tn), lambda l, j: (l, 0, j)),
        out_shape=jax.ShapeDtypeStruct((DEPTH, SUBLANES, N_MOD * D_MODEL), F32),
        compiler_params=_cparams(("arbitrary", "arbitrary")),
        name="modulation",
    )(c2, mod_w, mod_b.reshape(DEPTH, 1, N_MOD * D_MODEL))
    return out[:, :2].reshape(DEPTH, 2, N_MOD, D_MODEL)


def _ffn_body(s_ref, mod_ref, gpre_ref, gpost_ref, wg_ref, wu_ref, wd_ref, o_ref, *, slot):
    s = s_ref[...]
    h = _pre(s, mod_ref, slot, gpre_ref).astype(BF16)
    g = jnp.dot(h, wg_ref[...], preferred_element_type=F32)
    u = jnp.dot(h, wu_ref[...], preferred_element_type=F32)
    act = (g * _sigmoid(g)) * u
    y = jnp.dot(act.astype(BF16), wd_ref[...], preferred_element_type=F32)
    o_ref[...] = _post(s, y, mod_ref, slot, gpost_ref, FFN_RES)


def _ffn(s, mod, gpre, gpost, wg, wu, wd, slot):
    n = s.shape[0]
    return pl.pallas_call(
        functools.partial(_ffn_body, slot=slot),
        grid=(n // ROW_TILE,),
        in_specs=[_row_spec(), _mod_spec(n), _full(gpre), _full(gpost),
                  _full(wg), _full(wu), _full(wd)],
        out_specs=_row_spec(),
        out_shape=jax.ShapeDtypeStruct(s.shape, F32),
        compiler_params=_cparams(("arbitrary",)),
        name="ffn",
    )(s, mod, gpre, gpost, wg, wu, wd)


def _rwkv_proj_body(*refs, vres):
    (s_ref, sp_ref, sn_ref, mod_ref, gpre_ref, mu_ref, wr_ref, wk_ref, wv_ref,
     w0_ref, w1_ref, w2_ref, a0_ref, a1_ref, a2_ref, g1_ref, g2_ref) = refs[:17]
    refs = refs[17:]
    if vres:
        v0_ref, v1_ref, v2_ref, vf_ref = refs[:4]
        refs = refs[4:]
    r_ref, k_ref, v_ref, g_ref, lw_ref, a_ref = refs

    first, last = _stream_edges()
    h = _pre(s_ref[...], mod_ref, 1, gpre_ref)
    hp = _pre(sp_ref[SUBLANES - 1:SUBLANES, :], mod_ref, 1, gpre_ref)
    hn = _pre(sn_ref[0:1, :], mod_ref, 1, gpre_ref)
    hp = jnp.where(first, 0.0, hp)
    hn = jnp.where(last, 0.0, hn)
    row = lax.broadcasted_iota(jnp.int32, h.shape, 0)
    prev = jnp.where(row == 0, hp, pltpu.roll(h, 1, 0))
    nxt = jnp.where(row == ROW_TILE - 1, hn, pltpu.roll(h, ROW_TILE - 1, 0))
    xx = 0.5 * (prev + nxt) - h

    def mix(n):
        return h + xx * mu_ref[n:n + 1, :]

    r_ref[...] = _bdot(mix(0), wr_ref[...])
    k_ref[...] = _bdot(mix(2), wk_ref[...])
    xvb = mix(3).astype(BF16)
    v = jnp.dot(xvb, wv_ref[...], preferred_element_type=F32)
    if vres:
        gate = _sigmoid(v0_ref[...] + _bdot(jnp.dot(xvb, v1_ref[...], preferred_element_type=F32),
                                            v2_ref[...]))
        v = v + (vf_ref[...] - v) * gate
    v_ref[...] = v
    g_ref[...] = _bdot(_sigmoid(_bdot(mix(5), g1_ref[...])), g2_ref[...])
    lw = w0_ref[...] + _bdot(jnp.tanh(_bdot(mix(1), w1_ref[...])), w2_ref[...])
    logw = -math.exp(-0.5) * _sigmoid(lw)
    aa = _sigmoid(a0_ref[...] + _bdot(_bdot(mix(4), a1_ref[...]), a2_ref[...]))
    for d in range(2):
        lw_ref[d] = logw[:, d * D_MODEL:(d + 1) * D_MODEL]
        a_ref[d] = aa[:, d * D_MODEL:(d + 1) * D_MODEL]


def _lora_pair(w1, w2, w0):
    lora = w1.shape[-1]
    w1c = jnp.concatenate([w1[0], w1[1]], axis=1)
    z = jnp.zeros((lora, D_MODEL), w2.dtype)
    w2c = jnp.concatenate([jnp.concatenate([w2[0], z], axis=1),
                           jnp.concatenate([z, w2[1]], axis=1)], axis=0)
    return w1c.astype(BF16), w2c.astype(BF16), w0.reshape(1, 2 * D_MODEL)


def _pad_to(a, axis, size):
    pad = [(0, 0)] * a.ndim
    pad[axis] = (0, size - a.shape[axis])
    return jnp.pad(a, pad)


def _rwkv_proj(s, mod, gpre, p, v_first):
    n = s.shape[0]
    vres = v_first is not None
    w1c, w2c, w0c = _lora_pair(p['w1'], p['w2'], p['w0'])
    a1c, a2c, a0c = _lora_pair(p['a1'], p['a2'], p['a0'])
    g1 = _pad_to(p['g1'], 1, RW_GATE_LORA_PAD).astype(BF16)
    g2 = _pad_to(p['g2'], 0, RW_GATE_LORA_PAD).astype(BF16)
    prev_spec, next_spec = _halo_specs(n)
    args = [s, s, s, mod, gpre, p['mu'], p['w_r'].astype(BF16), p['w_k'].astype(BF16),
            p['w_v'].astype(BF16), w0c, w1c, w2c, a0c, a1c, a2c, g1, g2]
    specs = [_row_spec(), prev_spec, next_spec, _mod_spec(n)] + [_full(a) for a in args[4:]]
    if vres:
        extra = [p['v0'].reshape(1, D_MODEL), _pad_to(p['v1'], 1, LORA_PAD).astype(BF16),
                 _pad_to(p['v2'], 0, LORA_PAD).astype(BF16)]
        args += extra + [v_first]
        specs += [_full(a) for a in extra] + [_row_spec()]
    row_out = jax.ShapeDtypeStruct((n, D_MODEL), F32)
    dir_out = jax.ShapeDtypeStruct((2, n, D_MODEL), F32)
    dir_spec = pl.BlockSpec((2, ROW_TILE, D_MODEL), lambda i: (0, i, 0))
    return pl.pallas_call(
        functools.partial(_rwkv_proj_body, vres=vres),
        grid=(n // ROW_TILE,),
        in_specs=specs,
        out_specs=[_row_spec()] * 4 + [dir_spec] * 2,
        out_shape=[row_out] * 4 + [dir_out] * 2,
        compiler_params=_cparams(("arbitrary",)),
        name="rwkv_proj",
    )(*args)


WKV_PAIR = LANES // RW_HEAD
WKV_ROWS = WKV_PAIR * WKV_CHUNK
WKV_GROUPS = D_MODEL // LANES
WKV_INV_LEVELS = tuple(range(1, WKV_CHUNK.bit_length() - 1))


def _wkv_masks(sgn):
    c = WKV_CHUNK
    ti = lax.broadcasted_iota(jnp.int32, (c, c), 0)
    si = lax.broadcasted_iota(jnp.int32, (c, c), 1)
    incl_c = ((ti - si) * sgn >= 0).astype(F32)
    rows = lax.broadcasted_iota(jnp.int32, (WKV_ROWS, LANES), 0)
    lanes = lax.broadcasted_iota(jnp.int32, (WKV_ROWS, LANES), 1)
    keep = (rows < c) == (lanes < RW_HEAD)
    tr = lax.broadcasted_iota(jnp.int32, (WKV_ROWS, WKV_ROWS), 0)
    sr = lax.broadcasted_iota(jnp.int32, (WKV_ROWS, WKV_ROWS), 1)
    same = (tr < c) == (sr < c)
    strict = jnp.logical_and(same, (tr - sr) * sgn > 0)
    incl = jnp.logical_and(same, (tr - sr) * sgn >= 0)
    eye = (tr == sr).astype(F32)
    blk = lambda x, b: lax.shift_right_logical(x, b)
    pairs = [blk(tr, 1) == blk(sr, 1)]
    for b in WKV_INV_LEVELS:
        pairs.append(jnp.logical_and(blk(tr, b + 1) == blk(sr, b + 1), blk(tr, b) != blk(sr, b)))
    return incl_c, keep, strict, incl, eye, pairs


def _dot_nt(x, y):
    return lax.dot_general(x.astype(BF16), y.astype(BF16), (((1,), (1,)), ((), ())),
                           preferred_element_type=F32)


def _pdot(x, y, prec):
    if prec is None:
        return _bdot(x, y)
    return jnp.dot(x, y, precision=prec, preferred_element_type=F32)


def _wkv_group(lw, cum, a, k, v, r, k_k, k_a, r_k, ln_w, ln_b, st_ref, o_ref, masks,
               prec_inv, prec_state):
    c = WKV_CHUNK
    _, keep, strict, incl, eye, pairs = masks
    head_a = lax.broadcasted_iota(jnp.int32, (c, LANES), 1) < RW_HEAD

    def head_sum(x):
        sa = jnp.sum(jnp.where(head_a, x, 0.0), axis=-1, keepdims=True)
        sb = jnp.sum(jnp.where(head_a, 0.0, x), axis=-1, keepdims=True)
        return jnp.where(head_a, sa, sb)

    kkf = k * k_k
    kap = kkf * lax.rsqrt(jnp.maximum(head_sum(kkf * kkf), 1e-24))
    kd = k * (1.0 + (a - 1.0) * k_a)
    b = kap * a

    tot = jnp.sum(lw, axis=0, keepdims=True)
    e_neg = jnp.exp(-cum)
    e_rem = jnp.exp(tot - cum)
    kt = kap * jnp.exp(cum - lw)
    rt = r * jnp.exp(cum)
    g_c = jnp.exp(tot)

    def stack(x):
        return jnp.where(keep, jnp.concatenate([x, x], axis=0), 0.0)

    kt_s, rt_s, v_s = stack(kt), stack(rt), stack(v)
    kh_s, bh_s = stack(kd * e_neg), stack(b * e_neg)
    kb_s, bb_s = stack(kd * e_rem), stack(b * e_rem)
    kbt, bbt = kb_s.T.astype(BF16), bb_s.T.astype(BF16)
    bonus = head_sum(r * kd * r_k) * v
    yield

    r2 = WKV_ROWS
    prod = _dot_nt(jnp.concatenate([kt_s, rt_s], axis=0), jnp.concatenate([kh_s, bh_s], axis=0))
    l_k = jnp.where(strict, prod[:r2, :r2], 0.0)
    l_b = jnp.where(strict, prod[:r2, r2:], 0.0)
    p_k = jnp.where(incl, prod[r2:, :r2], 0.0).astype(BF16)
    p_b = jnp.where(incl, prod[r2:, r2:], 0.0).astype(BF16)
    yield

    lkv = _bdot(l_k, v_s)
    t_inv = eye - jnp.where(pairs[0], l_b, 0.0)
    for lvl in range(1, len(pairs)):
        x = _pdot(jnp.where(pairs[lvl], l_b, 0.0), t_inv, prec_inv)
        yield
        t_inv = t_inv - _pdot(t_inv, x, prec_inv)
        yield

    wu = _bdot(t_inv, jnp.concatenate([kt_s, lkv], axis=1))
    yield
    wu = wu.astype(BF16)
    pbwu = jnp.dot(p_b, wu, preferred_element_type=F32)
    q_m = rt_s - pbwu[:, :LANES]
    y_0 = _bdot(p_k, v_s) - pbwu[:, LANES:]
    yield
    bbwu = jnp.dot(bbt, wu, preferred_element_type=F32)
    g_m = eye * g_c - bbwu[:, :LANES]
    h_m = _bdot(kbt, v_s) - bbwu[:, LANES:]
    yield

    a_state = st_ref[...]
    y_s = _pdot(q_m, a_state, prec_state) + y_0
    st_ref[...] = _pdot(g_m, a_state, prec_state) + h_m
    y = y_s[:c] + y_s[c:]

    mu = head_sum(y) * (1.0 / RW_HEAD)
    yc = y - mu
    var = head_sum(yc * yc) * (1.0 / RW_HEAD)
    yn = yc * lax.rsqrt(var + RW_GN_EPS) * ln_w + ln_b
    o_ref[...] = yn + bonus


def _wkv_body(lw_ref, a_ref, k_ref, v_ref, r_ref, kk_ref, ka_ref, rk_ref, lnw_ref, lnb_ref,
              o_ref, st_ref, *, prec_inv, prec_state):
    @pl.when(pl.program_id(1) == 0)
    def _():
        st_ref[...] = jnp.zeros_like(st_ref)

    fwd = pl.program_id(0) == 0
    masks = _wkv_masks(1 - 2 * pl.program_id(0))
    groups = []
    for i in range(WKV_STEP_CHUNKS):
        first = jnp.where(fwd, i, WKV_STEP_CHUNKS - 1 - i) * WKV_CHUNK
        rows = pl.ds(pl.multiple_of(first, WKV_CHUNK), WKV_CHUNK)
        cum = jnp.dot(masks[0], lw_ref[0, rows, :], precision=HIGHEST, preferred_element_type=F32)
        for g in range(WKV_GROUPS):
            sl = slice(g * LANES, (g + 1) * LANES)
            groups.append(_wkv_group(lw_ref[0, rows, sl], cum[:, sl], a_ref[0, rows, sl],
                                     k_ref[rows, sl], v_ref[rows, sl], r_ref[rows, sl],
                                     kk_ref[:, sl], ka_ref[:, sl], rk_ref[:, sl], lnw_ref[:, sl],
                                     lnb_ref[:, sl], st_ref.at[g], o_ref.at[0, rows, sl], masks,
                                     prec_inv, prec_state))
    while groups:
        groups = [gen for gen in groups if next(gen, "done") != "done"]


def _wkv(logw, a, k, v, r, p, prec_inv=None, prec_state=None):
    n = k.shape[0]
    blk = WKV_STEP_CHUNKS * WKV_CHUNK
    assert n % blk == 0 and CTX_LEN % blk == 0
    nb = n // blk
    nbc = CTX_LEN // blk
    nbx = nb - nbc

    def chunk(d, j):
        fwd = jnp.where(j < nbc, nbx + j, j - nbc)
        return jnp.where(d == 0, fwd, nb - 1 - j)

    dir_spec = pl.BlockSpec((1, blk, D_MODEL), lambda d, j: (d, chunk(d, j), 0))
    row_spec = pl.BlockSpec((blk, D_MODEL), lambda d, j: (chunk(d, j), 0))
    vecs = [p['k_k'].reshape(1, D_MODEL), p['k_a'].reshape(1, D_MODEL),
            p['r_k'].reshape(1, D_MODEL), p['ln_w'].reshape(1, D_MODEL),
            p['ln_b'].reshape(1, D_MODEL)]
    return pl.pallas_call(
        functools.partial(_wkv_body, prec_inv=prec_inv, prec_state=prec_state),
        grid=(2, nb),
        in_specs=[dir_spec, dir_spec, row_spec, row_spec, row_spec] + [_full(x) for x in vecs],
        out_specs=dir_spec,
        out_shape=jax.ShapeDtypeStruct((2, n, D_MODEL), F32),
        scratch_shapes=[pltpu.VMEM((WKV_GROUPS, WKV_ROWS, LANES), F32)],
        compiler_params=_cparams(("arbitrary", "arbitrary")),
        name="wkv_scan",
    )(logw, a, k, v, r, *vecs)


def _out_body(*refs, gated):
    if gated:
        o_ref, g_ref, w_ref, s_ref, mod_ref, gpost_ref, out_ref = refs
        y = (o_ref[0] + o_ref[1]) * g_ref[...]
    else:
        o_ref, w_ref, s_ref, mod_ref, gpost_ref, out_ref = refs
        y = o_ref[...]
    y = jnp.dot(y.astype(BF16), w_ref[...], preferred_element_type=F32)
    out_ref[...] = _post(s_ref[...], y, mod_ref, 1, gpost_ref, 1.0)


def _mixer_out(o, g, w_o, s, mod, gpost):
    n = s.shape[0]
    gated = g is not None
    if gated:
        args = [o, g, w_o, s, mod, gpost]
        specs = [pl.BlockSpec((2, ROW_TILE, D_MODEL), lambda i: (0, i, 0)), _row_spec(),
                 _full(w_o), _row_spec(), _mod_spec(n), _full(gpost)]
    else:
        args = [o, w_o, s, mod, gpost]
        specs = [_row_spec(), _full(w_o), _row_spec(), _mod_spec(n), _full(gpost)]
    return pl.pallas_call(
        functools.partial(_out_body, gated=gated),
        grid=(n // ROW_TILE,),
        in_specs=specs,
        out_specs=_row_spec(),
        out_shape=jax.ShapeDtypeStruct(s.shape, F32),
        compiler_params=_cparams(("arbitrary",)),
        name="mixer_out",
    )(*args)


def _mla_proj_body(s_ref, mod_ref, gpre_ref, wdq_ref, qn_ref, wq1_ref, wq2_ref, wdkv_ref, kvn_ref,
                   wkr_ref, wk_ref, wv_ref, place_ref, tq_ref, tk_ref, q_ref, k_ref, v_ref):
    h = _pre(s_ref[...], mod_ref, 1, gpre_ref).astype(BF16)
    cq = jnp.dot(h, wdq_ref[...], preferred_element_type=F32)
    qn = _rms(cq, qn_ref[...]).astype(BF16)
    q1 = jnp.dot(qn, wq1_ref[...], preferred_element_type=F32)
    q2 = jnp.dot(qn, wq2_ref[...], preferred_element_type=F32)
    ckv = jnp.dot(h, wdkv_ref[...], preferred_element_type=F32)
    kvn = _rms(ckv, kvn_ref[...]).astype(BF16)
    kr = jnp.dot(h, wkr_ref[...], preferred_element_type=F32)
    krr = kr * tk_ref[0] + pltpu.roll(kr, LANES - MLA_ROPE, 1) * tk_ref[1]
    kk = (jnp.dot(kvn, wk_ref[...], preferred_element_type=F32)
          + jnp.dot(krr.astype(BF16), place_ref[...], preferred_element_type=F32))
    k_ref[...] = kk.astype(BF16)
    v_ref[...] = jnp.dot(kvn, wv_ref[...], preferred_element_type=F32).astype(BF16)
    cos = tq_ref[0]
    sin = tq_ref[1]
    for hd in range(MLA_HEADS):
        sl = slice(hd * MLA_QK_PAD, (hd + 1) * MLA_QK_PAD)
        q_ref[:, sl] = (q1[:, sl] * cos + q2[:, sl] * sin).astype(BF16)


def _rot_cols(w):
    half = MLA_ROPE // 4
    parts = []
    for ax in range(2):
        blk = w[..., ax * 2 * half:(ax + 1) * 2 * half]
        parts += [-blk[..., half:], blk[..., :half]]
    return jnp.concatenate(parts, axis=-1)


def _mla_tables(n):
    t = n - CTX_LEN
    pos = jnp.arange(t, dtype=jnp.int32)
    rowp = (pos // GRID_W).astype(F32)
    colp = (pos % GRID_W).astype(F32)
    axis_dim = MLA_ROPE // 2
    inv = ROPE_BASE ** (-jnp.arange(0, axis_dim, 2, dtype=F32) / axis_dim)
    ar = rowp[:, None] * inv
    ac = colp[:, None] * inv
    cosf = jnp.concatenate([jnp.cos(ar), jnp.cos(ar), jnp.cos(ac), jnp.cos(ac)], axis=-1)
    sinf = jnp.concatenate([jnp.sin(ar), jnp.sin(ar), jnp.sin(ac), jnp.sin(ac)], axis=-1)
    cosf = jnp.concatenate([cosf, jnp.ones((CTX_LEN, MLA_ROPE), F32)], axis=0)
    sinf = jnp.concatenate([sinf, jnp.zeros((CTX_LEN, MLA_ROPE), F32)], axis=0)
    scale = (MLA_NOPE + MLA_ROPE) ** -0.5 * math.log2(math.e)
    zq = jnp.zeros((n, MLA_QK_PAD - MLA_NOPE - MLA_ROPE), F32)
    tq = jnp.stack([jnp.concatenate([jnp.ones((n, MLA_NOPE), F32), cosf, zq], axis=-1),
                    jnp.concatenate([jnp.zeros((n, MLA_NOPE), F32), sinf, zq], axis=-1)]) * scale
    zk = jnp.zeros((n, LANES - MLA_ROPE), F32)
    tk = jnp.stack([jnp.concatenate([cosf, zk], axis=-1), jnp.concatenate([sinf, zk], axis=-1)])
    return tq, tk


def _mla_proj(s, mod, gpre, w_dq, q_norm, w_uq, w_dkv, kv_norm, w_ukv):
    n = s.shape[0]
    hq = MLA_NOPE + MLA_ROPE
    wq = w_uq.reshape(MLA_Q_RANK, MLA_HEADS, hq)
    zq = jnp.zeros((MLA_Q_RANK, MLA_HEADS, MLA_QK_PAD - hq), w_uq.dtype)
    wq1 = jnp.concatenate([wq, zq], axis=-1).reshape(MLA_Q_RANK, MLA_HEADS * MLA_QK_PAD)
    wq2 = jnp.concatenate([jnp.zeros_like(wq[..., :MLA_NOPE]), _rot_cols(wq[..., MLA_NOPE:]), zq],
                          axis=-1).reshape(MLA_Q_RANK, MLA_HEADS * MLA_QK_PAD)
    wkv = w_ukv.reshape(MLA_KV_RANK, MLA_HEADS, MLA_NOPE + MLA_V)
    wk = jnp.concatenate([wkv[..., :MLA_NOPE],
                          jnp.zeros((MLA_KV_RANK, MLA_HEADS, MLA_QK_PAD - MLA_NOPE), w_ukv.dtype)],
                         axis=-1).reshape(MLA_KV_RANK, MLA_HEADS * MLA_QK_PAD)
    wv = wkv[..., MLA_NOPE:].reshape(MLA_KV_RANK, MLA_HEADS * MLA_V)
    w_rope = w_dkv[:, MLA_KV_RANK:]
    wkr = jnp.concatenate([w_rope, _rot_cols(w_rope),
                           jnp.zeros((D_MODEL, LANES - 2 * MLA_ROPE), w_dkv.dtype)], axis=-1)
    src = jnp.arange(LANES)[:, None]
    dst = jnp.arange(MLA_HEADS * MLA_QK_PAD)[None, :]
    place = jnp.logical_and(src < MLA_ROPE, dst % MLA_QK_PAD == MLA_NOPE + src).astype(BF16)
    tq, tk = _mla_tables(n)
    weights = [w_dq.astype(BF16), q_norm.reshape(1, MLA_Q_RANK), wq1.astype(BF16), wq2.astype(BF16),
               w_dkv[:, :MLA_KV_RANK].astype(BF16), kv_norm.reshape(1, MLA_KV_RANK),
               wkr.astype(BF16), wk.astype(BF16), wv.astype(BF16), place]
    tab_spec = pl.BlockSpec((2, ROW_TILE, LANES), lambda i: (0, i, 0))
    return pl.pallas_call(
        _mla_proj_body,
        grid=(n // ROW_TILE,),
        in_specs=[_row_spec(), _mod_spec(n), _full(gpre)] + [_full(w) for w in weights]
                 + [tab_spec, tab_spec],
        out_specs=[_row_spec(MLA_HEADS * MLA_QK_PAD), _row_spec(MLA_HEADS * MLA_QK_PAD),
                   _row_spec(MLA_HEADS * MLA_V)],
        out_shape=[jax.ShapeDtypeStruct((n, MLA_HEADS * MLA_QK_PAD), BF16),
                   jax.ShapeDtypeStruct((n, MLA_HEADS * MLA_QK_PAD), BF16),
                   jax.ShapeDtypeStruct((n, MLA_HEADS * MLA_V), BF16)],
        compiler_params=_cparams(("arbitrary",)),
        name="mla_proj",
    )(s, mod, gpre, *weights, tq, tk)


MLA_PAIR = LANES // MLA_V


ATTN_ROW_BLOCK = 16


def _attn_body(q_ref, k_ref, v_ref, o_ref, m_ref, l_ref, acc_ref, s_ref, p_ref, al_ref, *, tk, nk):
    tq = q_ref.shape[0]
    m_ref[...] = jnp.full_like(m_ref, -jnp.inf)
    l_ref[...] = jnp.zeros_like(l_ref)
    acc_ref[...] = jnp.zeros_like(acc_ref)

    blocks = [slice(b * ATTN_ROW_BLOCK, (b + 1) * ATTN_ROW_BLOCK)
              for b in range(tq // ATTN_ROW_BLOCK)]
    cols = [slice(j * LANES, (j + 1) * LANES) for j in range(tk // LANES)]

    def key_rows(tile):
        return pl.ds(pl.multiple_of(tile * tk, tk), tk)

    def scores(tile, slot):
        for hd in range(MLA_PAIR):
            sl = slice(hd * MLA_QK_PAD, (hd + 1) * MLA_QK_PAD)
            s_ref[slot, hd] = lax.dot_general(q_ref[:, sl], k_ref[key_rows(tile), sl],
                                              (((1,), (1,)), ((), ())), preferred_element_type=F32)

    def softmax_pv(tile, slot):
        vv = v_ref[key_rows(tile), :]
        for hd in range(MLA_PAIR):
            for r in blocks:
                mx = s_ref[slot, hd, r, cols[0]]
                for cj in cols[1:]:
                    mx = jnp.maximum(mx, s_ref[slot, hd, r, cj])
                m_old = m_ref[hd, r, :]
                m_new = jnp.maximum(m_old, jnp.max(mx, axis=-1, keepdims=True))
                m_ref[hd, r, :] = m_new
                al_ref[hd, r, :] = jnp.exp2(m_old - m_new)
            for r in blocks:
                m_new = m_ref[hd, r, :]
                lsum = al_ref[hd, r, :] * l_ref[hd, r, :]
                for cj in cols:
                    p = jnp.exp2(s_ref[slot, hd, r, cj] - m_new)
                    lsum = lsum + p
                    p_ref[hd, r, cj] = p.astype(BF16)
                l_ref[hd, r, :] = lsum
            acc_ref[hd] = al_ref[hd] * acc_ref[hd] + jnp.dot(p_ref[hd], vv,
                                                             preferred_element_type=F32)

    def step(i, carry):
        scores(i, 0)
        softmax_pv(i, 0)
        return carry

    lax.fori_loop(0, nk, step, 0)
    lane = lax.broadcasted_iota(jnp.int32, o_ref.shape, 1)
    l0 = jnp.sum(l_ref[0], axis=-1, keepdims=True)
    l1 = jnp.sum(l_ref[1], axis=-1, keepdims=True)
    out = jnp.where(lane < MLA_V, acc_ref[0] / l0, acc_ref[1] / l1)
    o_ref[...] = out.astype(o_ref.dtype)


def _attention(q, k, v, q_row0, q_rows, k_row0, k_rows, tq, tk):
    assert q_row0 % tq == 0 and q_rows % tq == 0 and k_row0 % k_rows == 0 and k_rows % tk == 0
    q0 = q_row0 // tq
    k0 = k_row0 // k_rows
    pair_w = MLA_PAIR * MLA_QK_PAD
    return pl.pallas_call(
        functools.partial(_attn_body, tk=tk, nk=k_rows // tk),
        grid=(MLA_HEADS // MLA_PAIR, q_rows // tq),
        in_specs=[pl.BlockSpec((tq, pair_w), lambda p, i: (i + q0, p)),
                  pl.BlockSpec((k_rows, pair_w), lambda p, i: (k0, p)),
                  pl.BlockSpec((k_rows, LANES), lambda p, i: (k0, p))],
        out_specs=pl.BlockSpec((tq, LANES), lambda p, i: (i, p)),
        out_shape=jax.ShapeDtypeStruct((q_rows, MLA_HEADS * MLA_V), BF16),
        scratch_shapes=[pltpu.VMEM((MLA_PAIR, tq, LANES), F32), pltpu.VMEM((MLA_PAIR, tq, LANES), F32),
                        pltpu.VMEM((MLA_PAIR, tq, LANES), F32), pltpu.VMEM((1, MLA_PAIR, tq, tk), F32),
                        pltpu.VMEM((MLA_PAIR, tq, tk), BF16), pltpu.VMEM((MLA_PAIR, tq, LANES), F32)],
        compiler_params=_cparams(("arbitrary", "arbitrary")),
        name="mla_attention",
    )(q, k, v)


def _largest_divisor(n, candidates):
    for c in candidates:
        if n % c == 0:
            return c
    raise ValueError(f"no tile in {candidates} divides {n}")


def _pool_body(s_ref, sp_ref, sn_ref, mod_ref, gpre_ref, gpost_ref, w_ref, b_ref, sc_ref, o_ref):
    i = pl.program_id(0)
    nt = pl.num_programs(0)
    nct = CTX_LEN // ROW_TILE
    first, last = _stream_edges()
    s = s_ref[...]
    h = _pre(s, mod_ref, 1, gpre_ref)
    hp = jnp.where(first, 0.0, _pre(sp_ref[...], mod_ref, 1, gpre_ref))
    hn = jnp.where(last, 0.0, _pre(sn_ref[...], mod_ref, 1, gpre_ref))
    ext = jnp.concatenate([hp, h, hn], axis=0)
    er = ROW_TILE + 2 * POOL_HALO
    in_ctx = i >= nt - nct
    t0 = jnp.where(in_ctx, i - (nt - nct), i) * ROW_TILE
    t_len = jnp.where(in_ctx, nct, nt - nct) * ROW_TILE
    t = t0 + lax.broadcasted_iota(jnp.int32, (ROW_TILE, 1), 0)

    outs = []
    run = ext
    width = 1
    for gi, win in enumerate(POOL_WINDOWS):
        while width < win:
            run = run + pltpu.roll(run, width, 0)
            width *= 2
        ahead = win // 2 - 1
        grp = run[:, :POOL_GROUP]
        if ahead:
            grp = pltpu.roll(grp, er - ahead, 0)
        wsum = grp[POOL_HALO:POOL_HALO + ROW_TILE]
        cnt = (jnp.minimum(t + win // 2, t_len) - jnp.maximum(t - win // 2, 0)).astype(F32)
        diff = wsum / cnt - h[:, gi * POOL_GROUP:(gi + 1) * POOL_GROUP]
        outs.append(_bdot(diff, w_ref[gi]) + b_ref[gi:gi + 1, :])
        run = run[:, POOL_GROUP:]
    y = jnp.concatenate(outs, axis=-1) * sc_ref[...]
    o_ref[...] = _post(s, y, mod_ref, 1, gpost_ref, 1.0)


def _pool(s, mod, gpre, gpost, w, b, scale):
    n = s.shape[0]
    prev_spec, next_spec = _halo_specs(n)
    sc = scale.reshape(1, D_MODEL)
    wb = w.astype(BF16)
    return pl.pallas_call(
        _pool_body,
        grid=(n // ROW_TILE,),
        in_specs=[_row_spec(), prev_spec, next_spec, _mod_spec(n), _full(gpre), _full(gpost),
                  _full(wb), _full(b), _full(sc)],
        out_specs=_row_spec(),
        out_shape=jax.ShapeDtypeStruct(s.shape, F32),
        compiler_params=_cparams(("arbitrary",)),
        name="pool_mixer",
    )(s, s, s, mod, gpre, gpost, wb, b, sc)


def kernel(x, c, ctx, c_ctx, mod_w, mod_b, norm_pre, norm_post, ffn_w_gate, ffn_w_up, ffn_w_down,
           rw_mu, rw_w_r, rw_w_k, rw_w_v, rw_w_o, rw_w0, rw_w1, rw_w2, rw_a0, rw_a1, rw_a2,
           rw_v0, rw_v1, rw_v2, rw_g1, rw_g2, rw_k_k, rw_k_a, rw_r_k, rw_ln_w, rw_ln_b,
           mla_w_dq, mla_q_norm, mla_w_uq, mla_w_dkv, mla_kv_norm, mla_w_ukv, mla_w_o,
           pool_w, pool_b, pool_scale):
    batch, t, d = x.shape
    assert batch == 1 and d == D_MODEL and ctx.shape == (1, CTX_LEN, D_MODEL)
    assert t % ROW_TILE == 0 and CTX_LEN % ROW_TILE == 0 and t % GRID_W == 0
    n = t + CTX_LEN
    s = jnp.concatenate([x[0], ctx[0]], axis=0)
    c2 = jnp.concatenate([c_ctx[None], c, jnp.zeros((SUBLANES - 2, D_MODEL), F32)], axis=0)
    mod_all = _modulation(c2, mod_w, mod_b)
    v_first = None
    for i in range(DEPTH):
        kind, j = i % 3, i // 3
        mod = mod_all[i]
        gpre, gpost = norm_pre[i], norm_post[i]
        s = _ffn(s, mod, gpre, gpost, ffn_w_gate[i, 0].astype(BF16), ffn_w_up[i, 0].astype(BF16),
                 ffn_w_down[i, 0].astype(BF16), 0)
        if kind == 0:
            p = {'mu': rw_mu[j], 'w_r': rw_w_r[j], 'w_k': rw_w_k[j], 'w_v': rw_w_v[j],
                 'w0': rw_w0[j], 'w1': rw_w1[j], 'w2': rw_w2[j],
                 'a0': rw_a0[j], 'a1': rw_a1[j], 'a2': rw_a2[j],
                 'g1': rw_g1[j], 'g2': rw_g2[j], 'k_k': rw_k_k[j], 'k_a': rw_k_a[j],
                 'r_k': rw_r_k[j], 'ln_w': rw_ln_w[j], 'ln_b': rw_ln_b[j]}
            if j > 0:
                p['v0'], p['v1'], p['v2'] = rw_v0[j - 1], rw_v1[j - 1], rw_v2[j - 1]
            r, k, v, g, logw, a = _rwkv_proj(s, mod, gpre, p, v_first if j > 0 else None)
            if j == 0:
                v_first = v
            o = _wkv(logw, a, k, v, r, p)
            s = _mixer_out(o, g, rw_w_o[j].astype(BF16), s, mod, gpost)
        elif kind == 1:
            q, k, v = _mla_proj(s, mod, gpre, mla_w_dq[j], mla_q_norm[j], mla_w_uq[j],
                                mla_w_dkv[j], mla_kv_norm[j], mla_w_ukv[j])
            tq = _largest_divisor(t, (512, 256))
            tk = _largest_divisor(n, (1280, 256))
            o_x = _attention(q, k, v, 0, t, 0, n, tq, tk)
            o_c = _attention(q, k, v, t, CTX_LEN, t, CTX_LEN, CTX_LEN, CTX_LEN)
            o = jnp.concatenate([o_x, o_c], axis=0)
            s = _mixer_out(o, None, mla_w_o[j].astype(BF16), s, mod, gpost)
        else:
            s = _pool(s, mod, gpre, gpost, pool_w[j], pool_b[j], pool_scale[j])
        s = _ffn(s, mod, gpre, gpost, ffn_w_gate[i, 1].astype(BF16), ffn_w_up[i, 1].astype(BF16),
                 ffn_w_down[i, 1].astype(BF16), 2)
    return s[:t][None]
```

```python
import functools
import math

import jax
import jax.numpy as jnp
from jax import lax
from jax.experimental import pallas as pl
from jax.experimental.pallas import tpu as pltpu

F32 = jnp.float32
BF16 = jnp.bfloat16
HIGHEST = lax.Precision.HIGHEST

D_MODEL = 1024
DEPTH = 4
CTX_LEN = 256
GRID_W = 64
N_MOD = 9
FFN_RES = 0.5
NORM_EPS = 1e-6

RW_HEAD = 64
RW_GN_EPS = 64e-5
RW_GATE_LORA_PAD = 256
LORA_PAD = 128

MLA_HEADS = 16
MLA_NOPE = 64
MLA_ROPE = 32
MLA_V = 64
MLA_Q_RANK = 384
MLA_KV_RANK = 256
ROPE_BASE = 10000.0
MLA_QK_PAD = 128

POOL_WINDOWS = (2, 4, 8, 16)
POOL_GROUP = D_MODEL // len(POOL_WINDOWS)
POOL_HALO = 8

LANES = 128
SUBLANES = 8
ROW_TILE = 256
WKV_CHUNK = 64
WKV_STEP_CHUNKS = 2
VMEM_LIMIT = 56 * 1024 * 1024


def _cparams(sem):
    return pltpu.CompilerParams(dimension_semantics=sem, vmem_limit_bytes=VMEM_LIMIT)


def _full(a):
    nd = a.ndim
    return pl.BlockSpec(a.shape, lambda *_: (0,) * nd)


def _sigmoid(x):
    return 1.0 / (1.0 + jnp.exp(-x))


def _rms(x, g):
    return x * lax.rsqrt(jnp.mean(x * x, axis=-1, keepdims=True) + NORM_EPS) * g


def _pre(s, mod_ref, slot, gpre_ref):
    shift = mod_ref[0, 3 * slot:3 * slot + 1, :]
    scale = mod_ref[0, 3 * slot + 1:3 * slot + 2, :]
    return _rms(s, gpre_ref[slot:slot + 1, :]) * (1.0 + scale) + shift


def _post(s, y, mod_ref, slot, gpost_ref, weight):
    gate = mod_ref[0, 3 * slot + 2:3 * slot + 3, :]
    return s + (weight * gate) * _rms(y, gpost_ref[slot:slot + 1, :])


def _bdot(a, b):
    return jnp.dot(a.astype(BF16), b.astype(BF16), preferred_element_type=F32)


def _row_spec(width=D_MODEL):
    return pl.BlockSpec((ROW_TILE, width), lambda i: (i, 0))


def _mod_spec(n_rows):
    nxt = (n_rows - CTX_LEN) // ROW_TILE
    return pl.BlockSpec((1, N_MOD, D_MODEL), lambda i: (jnp.where(i >= nxt, 0, 1), 0, 0))


def _halo_specs(n_rows):
    per = ROW_TILE // SUBLANES
    nblk = n_rows // SUBLANES
    prev = pl.BlockSpec((SUBLANES, D_MODEL), lambda i: (jnp.maximum(i * per - 1, 0), 0))
    nxt = pl.BlockSpec((SUBLANES, D_MODEL), lambda i: (jnp.minimum((i + 1) * per, nblk - 1), 0))
    return prev, nxt


def _stream_edges():
    i = pl.program_id(0)
    nt = pl.num_programs(0)
    nxt = nt - CTX_LEN // ROW_TILE
    first = jnp.logical_or(i == 0, i == nxt)
    last = jnp.logical_or(i == nxt - 1, i == nt - 1)
    return first, last


def _mod_body(c_ref, w_ref, b_ref, o_ref):
    c = c_ref[...]
    s = c * _sigmoid(c)
    o_ref[0] = _bdot(s, w_ref[0]) + b_ref[0]


def _modulation(c2, mod_w, mod_b):
    tn = D_MODEL
    out = pl.pallas_call(
        _mod_body,
        grid=(DEPTH, N_MOD * D_MODEL // tn),
        in_specs=[pl.BlockSpec((SUBLANES, D_MODEL), lambda l, j: (0, 0)),
                  pl.BlockSpec((1, D_MODEL, tn), lambda l, j: (l, 0, j)),
                  pl.BlockSpec((1, 1, tn), lambda l, j: (l, 0, j))],
        out_specs=pl.BlockSpec((1, SUBLANES, tn), lambda l, j: (l, 0, j)),
        out_shape=jax.ShapeDtypeStruct((DEPTH, SUBLANES, N_MOD * D_MODEL), F32),
        compiler_params=_cparams(("arbitrary", "arbitrary")),
        name="modulation",
    )(c2, mod_w, mod_b.reshape(DEPTH, 1, N_MOD * D_MODEL))
    return out[:, :2].reshape(DEPTH, 2, N_MOD, D_MODEL)


def _ffn_body(s_ref, mod_ref, gpre_ref, gpost_ref, wg_ref, wu_ref, wd_ref, o_ref, *, slot):
    s = s_ref[...]
    h = _pre(s, mod_ref, slot, gpre_ref).astype(BF16)
    g = jnp.dot(h, wg_ref[...], preferred_element_type=F32)
    u = jnp.dot(h, wu_ref[...], preferred_element_type=F32)
    act = (g * _sigmoid(g)) * u
    y = jnp.dot(act.astype(BF16), wd_ref[...], preferred_element_type=F32)
    o_ref[...] = _post(s, y, mod_ref, slot, gpost_ref, FFN_RES)


def _ffn(s, mod, gpre, gpost, wg, wu, wd, slot):
    n = s.shape[0]
    return pl.pallas_call(
        functools.partial(_ffn_body, slot=slot),
        grid=(n // ROW_TILE,),
        in_specs=[_row_spec(), _mod_spec(n), _full(gpre), _full(gpost),
                  _full(wg), _full(wu), _full(wd)],
        out_specs=_row_spec(),
        out_shape=jax.ShapeDtypeStruct(s.shape, F32),
        compiler_params=_cparams(("arbitrary",)),
        name="ffn",
    )(s, mod, gpre, gpost, wg, wu, wd)


def _rwkv_proj_body(*refs, vres):
    (s_ref, sp_ref, sn_ref, mod_ref, gpre_ref, mu_ref, wr_ref, wk_ref, wv_ref,
     w0_ref, w1_ref, w2_ref, a0_ref, a1_ref, a2_ref, g1_ref, g2_ref) = refs[:17]
    refs = refs[17:]
    if vres:
        v0_ref, v1_ref, v2_ref, vf_ref = refs[:4]
        refs = refs[4:]
    r_ref, k_ref, v_ref, g_ref, lw_ref, a_ref = refs

    first, last = _stream_edges()
    h = _pre(s_ref[...], mod_ref, 1, gpre_ref)
    hp = _pre(sp_ref[SUBLANES - 1:SUBLANES, :], mod_ref, 1, gpre_ref)
    hn = _pre(sn_ref[0:1, :], mod_ref, 1, gpre_ref)
    hp = jnp.where(first, 0.0, hp)
    hn = jnp.where(last, 0.0, hn)
    row = lax.broadcasted_iota(jnp.int32, h.shape, 0)
    prev = jnp.where(row == 0, hp, pltpu.roll(h, 1, 0))
    nxt = jnp.where(row == ROW_TILE - 1, hn, pltpu.roll(h, ROW_TILE - 1, 0))
    xx = 0.5 * (prev + nxt) - h

    def mix(n):
        return h + xx * mu_ref[n:n + 1, :]

    r_ref[...] = _bdot(mix(0), wr_ref[...])
    k_ref[...] = _bdot(mix(2), wk_ref[...])
    xvb = mix(3).astype(BF16)
    v = jnp.dot(xvb, wv_ref[...], preferred_element_type=F32)
    if vres:
        gate = _sigmoid(v0_ref[...] + _bdot(jnp.dot(xvb, v1_ref[...], preferred_element_type=F32),
                                            v2_ref[...]))
        v = v + (vf_ref[...] - v) * gate
    v_ref[...] = v
    g_ref[...] = _bdot(_sigmoid(_bdot(mix(5), g1_ref[...])), g2_ref[...])
    lw = w0_ref[...] + _bdot(jnp.tanh(_bdot(mix(1), w1_ref[...])), w2_ref[...])
    logw = -math.exp(-0.5) * _sigmoid(lw)
    aa = _sigmoid(a0_ref[...] + _bdot(_bdot(mix(4), a1_ref[...]), a2_ref[...]))
    for d in range(2):
        lw_ref[d] = logw[:, d * D_MODEL:(d + 1) * D_MODEL]
        a_ref[d] = aa[:, d * D_MODEL:(d + 1) * D_MODEL]


def _lora_pair(w1, w2, w0):
    lora = w1.shape[-1]
    w1c = jnp.concatenate([w1[0], w1[1]], axis=1)
    z = jnp.zeros((lora, D_MODEL), w2.dtype)
    w2c = jnp.concatenate([jnp.concatenate([w2[0], z], axis=1),
                           jnp.concatenate([z, w2[1]], axis=1)], axis=0)
    return w1c.astype(BF16), w2c.astype(BF16), w0.reshape(1, 2 * D_MODEL)


def _pad_to(a, axis, size):
    pad = [(0, 0)] * a.ndim
    pad[axis] = (0, size - a.shape[axis])
    return jnp.pad(a, pad)


def _rwkv_proj(s, mod, gpre, p, v_first):
    n = s.shape[0]
    vres = v_first is not None
    w1c, w2c, w0c = _lora_pair(p['w1'], p['w2'], p['w0'])
    a1c, a2c, a0c = _lora_pair(p['a1'], p['a2'], p['a0'])
    g1 = _pad_to(p['g1'], 1, RW_GATE_LORA_PAD).astype(BF16)
    g2 = _pad_to(p['g2'], 0, RW_GATE_LORA_PAD).astype(BF16)
    prev_spec, next_spec = _halo_specs(n)
    args = [s, s, s, mod, gpre, p['mu'], p['w_r'].astype(BF16), p['w_k'].astype(BF16),
            p['w_v'].astype(BF16), w0c, w1c, w2c, a0c, a1c, a2c, g1, g2]
    specs = [_row_spec(), prev_spec, next_spec, _mod_spec(n)] + [_full(a) for a in args[4:]]
    if vres:
        extra = [p['v0'].reshape(1, D_MODEL), _pad_to(p['v1'], 1, LORA_PAD).astype(BF16),
                 _pad_to(p['v2'], 0, LORA_PAD).astype(BF16)]
        args += extra + [v_first]
        specs += [_full(a) for a in extra] + [_row_spec()]
    row_out = jax.ShapeDtypeStruct((n, D_MODEL), F32)
    dir_out = jax.ShapeDtypeStruct((2, n, D_MODEL), F32)
    dir_spec = pl.BlockSpec((2, ROW_TILE, D_MODEL), lambda i: (0, i, 0))
    return pl.pallas_call(
        functools.partial(_rwkv_proj_body, vres=vres),
        grid=(n // ROW_TILE,),
        in_specs=specs,
        out_specs=[_row_spec()] * 4 + [dir_spec] * 2,
        out_shape=[row_out] * 4 + [dir_out] * 2,
        compiler_params=_cparams(("arbitrary",)),
        name="rwkv_proj",
    )(*args)


WKV_PAIR = LANES // RW_HEAD
WKV_ROWS = WKV_PAIR * WKV_CHUNK
WKV_GROUPS = D_MODEL // LANES
WKV_INV_LEVELS = tuple(range(1, WKV_CHUNK.bit_length() - 1))


def _wkv_masks(sgn):
    c = WKV_CHUNK
    ti = lax.broadcasted_iota(jnp.int32, (c, c), 0)
    si = lax.broadcasted_iota(jnp.int32, (c, c), 1)
    incl_c = ((ti - si) * sgn >= 0).astype(F32)
    rows = lax.broadcasted_iota(jnp.int32, (WKV_ROWS, LANES), 0)
    lanes = lax.broadcasted_iota(jnp.int32, (WKV_ROWS, LANES), 1)
    keep = (rows < c) == (lanes < RW_HEAD)
    tr = lax.broadcasted_iota(jnp.int32, (WKV_ROWS, WKV_ROWS), 0)
    sr = lax.broadcasted_iota(jnp.int32, (WKV_ROWS, WKV_ROWS), 1)
    same = (tr < c) == (sr < c)
    strict = jnp.logical_and(same, (tr - sr) * sgn > 0)
    incl = jnp.logical_and(same, (tr - sr) * sgn >= 0)
    eye = (tr == sr).astype(F32)
    blk = lambda x, b: lax.shift_right_logical(x, b)
    pairs = [blk(tr, 1) == blk(sr, 1)]
    for b in WKV_INV_LEVELS:
        pairs.append(jnp.logical_and(blk(tr, b + 1) == blk(sr, b + 1), blk(tr, b) != blk(sr, b)))
    return incl_c, keep, strict, incl, eye, pairs


def _dot_nt(x, y):
    return lax.dot_general(x.astype(BF16), y.astype(BF16), (((1,), (1,)), ((), ())),
                           preferred_element_type=F32)


def _pdot(x, y, prec):
    if prec is None:
        return _bdot(x, y)
    return jnp.dot(x, y, precision=prec, preferred_element_type=F32)


def _wkv_group(lw, cum, a, k, v, r, k_k, k_a, r_k, ln_w, ln_b, st_ref, o_ref, masks,
               prec_inv, prec_state):
    c = WKV_CHUNK
    _, keep, strict, incl, eye, pairs = masks
    head_a = lax.broadcasted_iota(jnp.int32, (c, LANES), 1) < RW_HEAD

    def head_sum(x):
        sa = jnp.sum(jnp.where(head_a, x, 0.0), axis=-1, keepdims=True)
        sb = jnp.sum(jnp.where(head_a, 0.0, x), axis=-1, keepdims=True)
        return jnp.where(head_a, sa, sb)

    kkf = k * k_k
    kap = kkf * lax.rsqrt(jnp.maximum(head_sum(kkf * kkf), 1e-24))
    kd = k * (1.0 + (a - 1.0) * k_a)
    b = kap * a

    tot = jnp.sum(lw, axis=0, keepdims=True)
    e_neg = jnp.exp(-cum)
    e_rem = jnp.exp(tot - cum)
    kt = kap * jnp.exp(cum - lw)
    rt = r * jnp.exp(cum)
    g_c = jnp.exp(tot)

    def stack(x):
        return jnp.where(keep, jnp.concatenate([x, x], axis=0), 0.0)

    kt_s, rt_s, v_s = stack(kt), stack(rt), stack(v)
    kh_s, bh_s = stack(kd * e_neg), stack(b * e_neg)
    kb_s, bb_s = stack(kd * e_rem), stack(b * e_rem)
    kbt, bbt = kb_s.T.astype(BF16), bb_s.T.astype(BF16)
    bonus = head_sum(r * kd * r_k) * v
    yield

    r2 = WKV_ROWS
    prod = _dot_nt(jnp.concatenate([kt_s, rt_s], axis=0), jnp.concatenate([kh_s, bh_s], axis=0))
    l_k = jnp.where(strict, prod[:r2, :r2], 0.0)
    l_b = jnp.where(strict, prod[:r2, r2:], 0.0)
    p_k = jnp.where(incl, prod[r2:, :r2], 0.0).astype(BF16)
    p_b = jnp.where(incl, prod[r2:, r2:], 0.0).astype(BF16)
    yield

    lkv = _bdot(l_k, v_s)
    t_inv = eye - jnp.where(pairs[0], l_b, 0.0)
    for lvl in range(1, len(pairs)):
        x = _pdot(jnp.where(pairs[lvl], l_b, 0.0), t_inv, prec_inv)
        yield
        t_inv = t_inv - _pdot(t_inv, x, prec_inv)
        yield

    wu = _bdot(t_inv, jnp.concatenate([kt_s, lkv], axis=1))
    yield
    wu = wu.astype(BF16)
    pbwu = jnp.dot(p_b, wu, preferred_element_type=F32)
    q_m = rt_s - pbwu[:, :LANES]
    y_0 = _bdot(p_k, v_s) - pbwu[:, LANES:]
    yield
    bbwu = jnp.dot(bbt, wu, preferred_element_type=F32)
    g_m = eye * g_c - bbwu[:, :LANES]
    h_m = _bdot(kbt, v_s) - bbwu[:, LANES:]
    yield

    a_state = st_ref[...]
    y_s = _pdot(q_m, a_state, prec_state) + y_0
    st_ref[...] = _pdot(g_m, a_state, prec_state) + h_m
    y = y_s[:c] + y_s[c:]

    mu = head_sum(y) * (1.0 / RW_HEAD)
    yc = y - mu
    var = head_sum(yc * yc) * (1.0 / RW_HEAD)
    yn = yc * lax.rsqrt(var + RW_GN_EPS) * ln_w + ln_b
    o_ref[...] = yn + bonus


def _wkv_body(lw_ref, a_ref, k_ref, v_ref, r_ref, kk_ref, ka_ref, rk_ref, lnw_ref, lnb_ref,
              o_ref, st_ref, *, prec_inv, prec_state):
    @pl.when(pl.program_id(1) == 0)
    def _():
        st_ref[...] = jnp.zeros_like(st_ref)

    fwd = pl.program_id(0) == 0
    masks = _wkv_masks(1 - 2 * pl.program_id(0))
    groups = []
    for i in range(WKV_STEP_CHUNKS):
        first = jnp.where(fwd, i, WKV_STEP_CHUNKS - 1 - i) * WKV_CHUNK
        rows = pl.ds(pl.multiple_of(first, WKV_CHUNK), WKV_CHUNK)
        cum = jnp.dot(masks[0], lw_ref[0, rows, :], precision=HIGHEST, preferred_element_type=F32)
        for g in range(WKV_GROUPS):
            sl = slice(g * LANES, (g + 1) * LANES)
            groups.append(_wkv_group(lw_ref[0, rows, sl], cum[:, sl], a_ref[0, rows, sl],
                                     k_ref[rows, sl], v_ref[rows, sl], r_ref[rows, sl],
                                     kk_ref[:, sl], ka_ref[:, sl], rk_ref[:, sl], lnw_ref[:, sl],
                                     lnb_ref[:, sl], st_ref.at[g], o_ref.at[0, rows, sl], masks,
                                     prec_inv, prec_state))
    while groups:
        groups = [gen for gen in groups if next(gen, "done") != "done"]


def _wkv(logw, a, k, v, r, p, prec_inv=None, prec_state=None):
    n = k.shape[0]
    blk = WKV_STEP_CHUNKS * WKV_CHUNK
    assert n % blk == 0 and CTX_LEN % blk == 0
    nb = n // blk
    nbc = CTX_LEN // blk
    nbx = nb - nbc

    def chunk(d, j):
        fwd = jnp.where(j < nbc, nbx + j, j - nbc)
        return jnp.where(d == 0, fwd, nb - 1 - j)

    dir_spec = pl.BlockSpec((1, blk, D_MODEL), lambda d, j: (d, chunk(d, j), 0))
    row_spec = pl.BlockSpec((blk, D_MODEL), lambda d, j: (chunk(d, j), 0))
    vecs = [p['k_k'].reshape(1, D_MODEL), p['k_a'].reshape(1, D_MODEL),
            p['r_k'].reshape(1, D_MODEL), p['ln_w'].reshape(1, D_MODEL),
            p['ln_b'].reshape(1, D_MODEL)]
    return pl.pallas_call(
        functools.partial(_wkv_body, prec_inv=prec_inv, prec_state=prec_state),
        grid=(2, nb),
        in_specs=[dir_spec, dir_spec, row_spec, row_spec, row_spec] + [_full(x) for x in vecs],
        out_specs=dir_spec,
        out_shape=jax.ShapeDtypeStruct((2, n, D_MODEL), F32),
        scratch_shapes=[pltpu.VMEM((WKV_GROUPS, WKV_ROWS, LANES), F32)],
        compiler_params=_cparams(("arbitrary", "arbitrary")),
        name="wkv_scan",
    )(logw, a, k, v, r, *vecs)


def _out_body(*refs, gated):
    if gated:
        o_ref, g_ref, w_ref, s_ref, mod_ref, gpost_ref, out_ref = refs
        y = (o_ref[0] + o_ref[1]) * g_ref[...]
    else:
        o_ref, w_ref, s_ref, mod_ref, gpost_ref, out_ref = refs
        y = o_ref[...]
    y = jnp.dot(y.astype(BF16), w_ref[...], preferred_element_type=F32)
    out_ref[...] = _post(s_ref[...], y, mod_ref, 1, gpost_ref, 1.0)


def _mixer_out(o, g, w_o, s, mod, gpost):
    n = s.shape[0]
    gated = g is not None
    if gated:
        args = [o, g, w_o, s, mod, gpost]
        specs = [pl.BlockSpec((2, ROW_TILE, D_MODEL), lambda i: (0, i, 0)), _row_spec(),
                 _full(w_o), _row_spec(), _mod_spec(n), _full(gpost)]
    else:
        args = [o, w_o, s, mod, gpost]
        specs = [_row_spec(), _full(w_o), _row_spec(), _mod_spec(n), _full(gpost)]
    return pl.pallas_call(
        functools.partial(_out_body, gated=gated),
        grid=(n // ROW_TILE,),
        in_specs=specs,
        out_specs=_row_spec(),
        out_shape=jax.ShapeDtypeStruct(s.shape, F32),
        compiler_params=_cparams(("arbitrary",)),
        name="mixer_out",
    )(*args)


def _mla_proj_body(s_ref, mod_ref, gpre_ref, wdq_ref, qn_ref, wq1_ref, wq2_ref, wdkv_ref, kvn_ref,
                   wkr_ref, wk_ref, wv_ref, place_ref, tq_ref, tk_ref, q_ref, k_ref, v_ref):
    h = _pre(s_ref[...], mod_ref, 1, gpre_ref).astype(BF16)
    cq = jnp.dot(h, wdq_ref[...], preferred_element_type=F32)
    qn = _rms(cq, qn_ref[...]).astype(BF16)
    q1 = jnp.dot(qn, wq1_ref[...], preferred_element_type=F32)
    q2 = jnp.dot(qn, wq2_ref[...], preferred_element_type=F32)
    ckv = jnp.dot(h, wdkv_ref[...], preferred_element_type=F32)
    kvn = _rms(ckv, kvn_ref[...]).astype(BF16)
    kr = jnp.dot(h, wkr_ref[...], preferred_element_type=F32)
    krr = kr * tk_ref[0] + pltpu.roll(kr, LANES - MLA_ROPE, 1) * tk_ref[1]
    kk = (jnp.dot(kvn, wk_ref[...], preferred_element_type=F32)
          + jnp.dot(krr.astype(BF16), place_ref[...], preferred_element_type=F32))
    k_ref[...] = kk.astype(BF16)
    v_ref[0] = jnp.dot(kvn, wv_ref[...], preferred_element_type=F32).T.astype(BF16)
    cos = tq_ref[0]
    sin = tq_ref[1]
    for hd in range(MLA_HEADS):
        sl = slice(hd * MLA_QK_PAD, (hd + 1) * MLA_QK_PAD)
        q_ref[:, sl] = (q1[:, sl] * cos + q2[:, sl] * sin).astype(BF16)


def _rot_cols(w):
    half = MLA_ROPE // 4
    parts = []
    for ax in range(2):
        blk = w[..., ax * 2 * half:(ax + 1) * 2 * half]
        parts += [-blk[..., half:], blk[..., :half]]
    return jnp.concatenate(parts, axis=-1)


def _mla_tables(n):
    t = n - CTX_LEN
    pos = jnp.arange(t, dtype=jnp.int32)
    rowp = (pos // GRID_W).astype(F32)
    colp = (pos % GRID_W).astype(F32)
    axis_dim = MLA_ROPE // 2
    inv = ROPE_BASE ** (-jnp.arange(0, axis_dim, 2, dtype=F32) / axis_dim)
    ar = rowp[:, None] * inv
    ac = colp[:, None] * inv
    cosf = jnp.concatenate([jnp.cos(ar), jnp.cos(ar), jnp.cos(ac), jnp.cos(ac)], axis=-1)
    sinf = jnp.concatenate([jnp.sin(ar), jnp.sin(ar), jnp.sin(ac), jnp.sin(ac)], axis=-1)
    cosf = jnp.concatenate([cosf, jnp.ones((CTX_LEN, MLA_ROPE), F32)], axis=0)
    sinf = jnp.concatenate([sinf, jnp.zeros((CTX_LEN, MLA_ROPE), F32)], axis=0)
    scale = (MLA_NOPE + MLA_ROPE) ** -0.5 * math.log2(math.e)
    zq = jnp.zeros((n, MLA_QK_PAD - MLA_NOPE - MLA_ROPE), F32)
    tq = jnp.stack([jnp.concatenate([jnp.ones((n, MLA_NOPE), F32), cosf, zq], axis=-1),
                    jnp.concatenate([jnp.zeros((n, MLA_NOPE), F32), sinf, zq], axis=-1)]) * scale
    zk = jnp.zeros((n, LANES - MLA_ROPE), F32)
    tk = jnp.stack([jnp.concatenate([cosf, zk], axis=-1), jnp.concatenate([sinf, zk], axis=-1)])
    return tq, tk


def _mla_proj(s, mod, gpre, w_dq, q_norm, w_uq, w_dkv, kv_norm, w_ukv, key_tile):
    n = s.shape[0]
    per = key_tile // ROW_TILE
    hq = MLA_NOPE + MLA_ROPE
    wq = w_uq.reshape(MLA_Q_RANK, MLA_HEADS, hq)
    zq = jnp.zeros((MLA_Q_RANK, MLA_HEADS, MLA_QK_PAD - hq), w_uq.dtype)
    wq1 = jnp.concatenate([wq, zq], axis=-1).reshape(MLA_Q_RANK, MLA_HEADS * MLA_QK_PAD)
    wq2 = jnp.concatenate([jnp.zeros_like(wq[..., :MLA_NOPE]), _rot_cols(wq[..., MLA_NOPE:]), zq],
                          axis=-1).reshape(MLA_Q_RANK, MLA_HEADS * MLA_QK_PAD)
    wkv = w_ukv.reshape(MLA_KV_RANK, MLA_HEADS, MLA_NOPE + MLA_V)
    wk = jnp.concatenate([wkv[..., :MLA_NOPE],
                          jnp.zeros((MLA_KV_RANK, MLA_HEADS, MLA_QK_PAD - MLA_NOPE), w_ukv.dtype)],
                         axis=-1).reshape(MLA_KV_RANK, MLA_HEADS * MLA_QK_PAD)
    wv = wkv[..., MLA_NOPE:].reshape(MLA_KV_RANK, MLA_HEADS * MLA_V)
    w_rope = w_dkv[:, MLA_KV_RANK:]
    wkr = jnp.concatenate([w_rope, _rot_cols(w_rope),
                           jnp.zeros((D_MODEL, LANES - 2 * MLA_ROPE), w_dkv.dtype)], axis=-1)
    src = jnp.arange(LANES)[:, None]
    dst = jnp.arange(MLA_HEADS * MLA_QK_PAD)[None, :]
    place = jnp.logical_and(src < MLA_ROPE, dst % MLA_QK_PAD == MLA_NOPE + src).astype(BF16)
    tq, tk = _mla_tables(n)
    weights = [w_dq.astype(BF16), q_norm.reshape(1, MLA_Q_RANK), wq1.astype(BF16), wq2.astype(BF16),
               w_dkv[:, :MLA_KV_RANK].astype(BF16), kv_norm.reshape(1, MLA_KV_RANK),
               wkr.astype(BF16), wk.astype(BF16), wv.astype(BF16), place]
    tab_spec = pl.BlockSpec((2, ROW_TILE, LANES), lambda i: (0, i, 0))
    return pl.pallas_call(
        _mla_proj_body,
        grid=(n // ROW_TILE,),
        in_specs=[_row_spec(), _mod_spec(n), _full(gpre)] + [_full(w) for w in weights]
                 + [tab_spec, tab_spec],
        out_specs=[_row_spec(MLA_HEADS * MLA_QK_PAD), _row_spec(MLA_HEADS * MLA_QK_PAD),
                   pl.BlockSpec((1, MLA_HEADS * MLA_V, ROW_TILE), lambda i: (i // per, 0, i % per))],
        out_shape=[jax.ShapeDtypeStruct((n, MLA_HEADS * MLA_QK_PAD), BF16),
                   jax.ShapeDtypeStruct((n, MLA_HEADS * MLA_QK_PAD), BF16),
                   jax.ShapeDtypeStruct((n // key_tile, MLA_HEADS * MLA_V, key_tile), BF16)],
        compiler_params=_cparams(("arbitrary",)),
        name="mla_proj",
    )(s, mod, gpre, *weights, tq, tk)


MLA_PAIR = LANES // MLA_V


ATTN_KEY_BLOCK = 16


def _attn_body(q_ref, k_ref, vt_ref, o_ref, m_ref, l_ref, acc_ref, s_ref, p_ref, *, tk, nk):
    tq = q_ref.shape[0]
    m_ref[...] = jnp.full_like(m_ref, -jnp.inf)
    l_ref[...] = jnp.zeros_like(l_ref)
    acc_ref[...] = jnp.zeros_like(acc_ref)

    def scores(t, slot):
        rows = pl.ds(pl.multiple_of(t * tk, tk), tk)
        for hd in range(MLA_PAIR):
            sl = slice(hd * MLA_QK_PAD, (hd + 1) * MLA_QK_PAD)
            s_ref[slot, hd] = lax.dot_general(k_ref[rows, sl], q_ref[:, sl], (((1,), (1,)), ((), ())),
                                              preferred_element_type=F32)

    def softmax_pv(t, slot):
        for hd in range(MLA_PAIR):
            mx = s_ref[slot, hd, 0:SUBLANES, :]
            for j in range(1, tk // SUBLANES):
                mx = jnp.maximum(mx, s_ref[slot, hd, j * SUBLANES:(j + 1) * SUBLANES, :])
            m_old = m_ref[hd]
            m_new = jnp.maximum(m_old, jnp.max(mx, axis=0, keepdims=True))
            m_ref[hd] = m_new
            alpha = jnp.exp2(m_old - m_new)
            m_rows = jnp.broadcast_to(m_new, (ATTN_KEY_BLOCK, tq))
            lsum = alpha * l_ref[hd]
            for b in range(tk // ATTN_KEY_BLOCK):
                r = slice(b * ATTN_KEY_BLOCK, (b + 1) * ATTN_KEY_BLOCK)
                p = jnp.exp2(s_ref[slot, hd, r, :] - m_rows)
                lsum = lsum + (p[:SUBLANES] + p[SUBLANES:])
                p_ref[hd, r, :] = p.astype(BF16)
            l_ref[hd] = lsum
            vt = vt_ref[t, hd * MLA_V:(hd + 1) * MLA_V, :]
            acc_ref[hd] = alpha * acc_ref[hd] + jnp.dot(vt, p_ref[hd], preferred_element_type=F32)

    scores(0, 0)

    def pair(j, carry):
        t = 2 * j
        scores(t + 1, 1)
        softmax_pv(t, 0)
        scores(t + 2, 0)
        softmax_pv(t + 1, 1)
        return carry

    lax.fori_loop(0, (nk - 1) // 2, pair, 0)
    if nk % 2:
        softmax_pv(nk - 1, 0)
    else:
        scores(nk - 1, 1)
        softmax_pv(nk - 2, 0)
        softmax_pv(nk - 1, 1)
    outs = [acc_ref[hd] / jnp.sum(l_ref[hd], axis=0, keepdims=True) for hd in range(MLA_PAIR)]
    o_ref[...] = jnp.concatenate(outs, axis=0).T.astype(o_ref.dtype)


def _attention(q, k, vt, q_row0, q_rows, k_row0, k_rows, tq, tk):
    vtile = vt.shape[2]
    assert q_row0 % tq == 0 and q_rows % tq == 0 and k_row0 % k_rows == 0 and k_rows % tk == 0
    q0 = q_row0 // tq
    k0 = k_row0 // k_rows
    pair_w = MLA_PAIR * MLA_QK_PAD
    if k_rows % vtile == 0:
        assert tk == vtile
        vt_spec = pl.BlockSpec((k_rows // vtile, LANES, vtile), lambda p, i: (k_row0 // k_rows, p, 0))
    else:
        assert vtile % k_rows == 0 and tk == k_rows
        vt_spec = pl.BlockSpec((1, LANES, k_rows),
                               lambda p, i: (k_row0 // vtile, p, (k_row0 % vtile) // k_rows))
    return pl.pallas_call(
        functools.partial(_attn_body, tk=tk, nk=k_rows // tk),
        grid=(MLA_HEADS // MLA_PAIR, q_rows // tq),
        in_specs=[pl.BlockSpec((tq, pair_w), lambda p, i: (i + q0, p)),
                  pl.BlockSpec((k_rows, pair_w), lambda p, i: (k0, p)),
                  vt_spec],
        out_specs=pl.BlockSpec((tq, LANES), lambda p, i: (i, p)),
        out_shape=jax.ShapeDtypeStruct((q_rows, MLA_HEADS * MLA_V), BF16),
        scratch_shapes=[pltpu.VMEM((MLA_PAIR, 1, tq), F32), pltpu.VMEM((MLA_PAIR, SUBLANES, tq), F32),
                        pltpu.VMEM((MLA_PAIR, MLA_V, tq), F32), pltpu.VMEM((2, MLA_PAIR, tk, tq), F32),
                        pltpu.VMEM((MLA_PAIR, tk, tq), BF16)],
        compiler_params=_cparams(("arbitrary", "arbitrary")),
        name="mla_attention",
    )(q, k, vt)


def _largest_divisor(n, candidates):
    for c in candidates:
        if n % c == 0:
            return c
    raise ValueError(f"no tile in {candidates} divides {n}")


def _pool_body(s_ref, sp_ref, sn_ref, mod_ref, gpre_ref, gpost_ref, w_ref, b_ref, sc_ref, o_ref):
    i = pl.program_id(0)
    nt = pl.num_programs(0)
    nct = CTX_LEN // ROW_TILE
    first, last = _stream_edges()
    s = s_ref[...]
    h = _pre(s, mod_ref, 1, gpre_ref)
    hp = jnp.where(first, 0.0, _pre(sp_ref[...], mod_ref, 1, gpre_ref))
    hn = jnp.where(last, 0.0, _pre(sn_ref[...], mod_ref, 1, gpre_ref))
    ext = jnp.concatenate([hp, h, hn], axis=0)
    er = ROW_TILE + 2 * POOL_HALO
    in_ctx = i >= nt - nct
    t0 = jnp.where(in_ctx, i - (nt - nct), i) * ROW_TILE
    t_len = jnp.where(in_ctx, nct, nt - nct) * ROW_TILE
    t = t0 + lax.broadcasted_iota(jnp.int32, (ROW_TILE, 1), 0)

    outs = []
    run = ext
    width = 1
    for gi, win in enumerate(POOL_WINDOWS):
        while width < win:
            run = run + pltpu.roll(run, width, 0)
            width *= 2
        ahead = win // 2 - 1
        grp = run[:, :POOL_GROUP]
        if ahead:
            grp = pltpu.roll(grp, er - ahead, 0)
        wsum = grp[POOL_HALO:POOL_HALO + ROW_TILE]
        cnt = (jnp.minimum(t + win // 2, t_len) - jnp.maximum(t - win // 2, 0)).astype(F32)
        diff = wsum / cnt - h[:, gi * POOL_GROUP:(gi + 1) * POOL_GROUP]
        outs.append(_bdot(diff, w_ref[gi]) + b_ref[gi:gi + 1, :])
        run = run[:, POOL_GROUP:]
    y = jnp.concatenate(outs, axis=-1) * sc_ref[...]
    o_ref[...] = _post(s, y, mod_ref, 1, gpost_ref, 1.0)


def _pool(s, mod, gpre, gpost, w, b, scale):
    n = s.shape[0]
    prev_spec, next_spec = _halo_specs(n)
    sc = scale.reshape(1, D_MODEL)
    wb = w.astype(BF16)
    return pl.pallas_call(
        _pool_body,
        grid=(n // ROW_TILE,),
        in_specs=[_row_spec(), prev_spec, next_spec, _mod_spec(n), _full(gpre), _full(gpost),
                  _full(wb), _full(b), _full(sc)],
        out_specs=_row_spec(),
        out_shape=jax.ShapeDtypeStruct(s.shape, F32),
        compiler_params=_cparams(("arbitrary",)),
        name="pool_mixer",
    )(s, s, s, mod, gpre, gpost, wb, b, sc)


def kernel(x, c, ctx, c_ctx, mod_w, mod_b, norm_pre, norm_post, ffn_w_gate, ffn_w_up, ffn_w_down,
           rw_mu, rw_w_r, rw_w_k, rw_w_v, rw_w_o, rw_w0, rw_w1, rw_w2, rw_a0, rw_a1, rw_a2,
           rw_v0, rw_v1, rw_v2, rw_g1, rw_g2, rw_k_k, rw_k_a, rw_r_k, rw_ln_w, rw_ln_b,
           mla_w_dq, mla_q_norm, mla_w_uq, mla_w_dkv, mla_kv_norm, mla_w_ukv, mla_w_o,
           pool_w, pool_b, pool_scale):
    batch, t, d = x.shape
    assert batch == 1 and d == D_MODEL and ctx.shape == (1, CTX_LEN, D_MODEL)
    assert t % ROW_TILE == 0 and CTX_LEN % ROW_TILE == 0 and t % GRID_W == 0
    n = t + CTX_LEN
    s = jnp.concatenate([x[0], ctx[0]], axis=0)
    c2 = jnp.concatenate([c_ctx[None], c, jnp.zeros((SUBLANES - 2, D_MODEL), F32)], axis=0)
    mod_all = _modulation(c2, mod_w, mod_b)
    v_first = None
    for i in range(DEPTH):
        kind, j = i % 3, i // 3
        mod = mod_all[i]
        gpre, gpost = norm_pre[i], norm_post[i]
        s = _ffn(s, mod, gpre, gpost, ffn_w_gate[i, 0].astype(BF16), ffn_w_up[i, 0].astype(BF16),
                 ffn_w_down[i, 0].astype(BF16), 0)
        if kind == 0:
            p = {'mu': rw_mu[j], 'w_r': rw_w_r[j], 'w_k': rw_w_k[j], 'w_v': rw_w_v[j],
                 'w0': rw_w0[j], 'w1': rw_w1[j], 'w2': rw_w2[j],
                 'a0': rw_a0[j], 'a1': rw_a1[j], 'a2': rw_a2[j],
                 'g1': rw_g1[j], 'g2': rw_g2[j], 'k_k': rw_k_k[j], 'k_a': rw_k_a[j],
                 'r_k': rw_r_k[j], 'ln_w': rw_ln_w[j], 'ln_b': rw_ln_b[j]}
            if j > 0:
                p['v0'], p['v1'], p['v2'] = rw_v0[j - 1], rw_v1[j - 1], rw_v2[j - 1]
            r, k, v, g, logw, a = _rwkv_proj(s, mod, gpre, p, v_first if j > 0 else None)
            if j == 0:
                v_first = v
            o = _wkv(logw, a, k, v, r, p)
            s = _mixer_out(o, g, rw_w_o[j].astype(BF16), s, mod, gpost)
        elif kind == 1:
            tq = _largest_divisor(t, (512, 256))
            tk = _largest_divisor(n, (1280, 256))
            q, k, vt = _mla_proj(s, mod, gpre, mla_w_dq[j], mla_q_norm[j], mla_w_uq[j],
                                 mla_w_dkv[j], mla_kv_norm[j], mla_w_ukv[j], tk)
            o_x = _attention(q, k, vt, 0, t, 0, n, tq, tk)
            o_c = _attention(q, k, vt, t, CTX_LEN, t, CTX_LEN, CTX_LEN, CTX_LEN)
            o = jnp.concatenate([o_x, o_c], axis=0)
            s = _mixer_out(o, None, mla_w_o[j].astype(BF16), s, mod, gpost)
        else:
            s = _pool(s, mod, gpre, gpost, pool_w[j], pool_b[j], pool_scale[j])
        s = _ffn(s, mod, gpre, gpost, ffn_w_gate[i, 1].astype(BF16), ffn_w_up[i, 1].astype(BF16),
                 ffn_w_down[i, 1].astype(BF16), 2)
    return s[:t][None]
```

```python
import functools
import math

import jax
import jax.numpy as jnp
from jax import lax
from jax.experimental import pallas as pl
from jax.experimental.pallas import tpu as pltpu

F32 = jnp.float32
BF16 = jnp.bfloat16
HIGHEST = lax.Precision.HIGHEST

D_MODEL = 1024
DEPTH = 4
CTX_LEN = 256
GRID_W = 64
N_MOD = 9
FFN_RES = 0.5
NORM_EPS = 1e-6

RW_HEAD = 64
RW_GN_EPS = 64e-5
RW_GATE_LORA_PAD = 256
LORA_PAD = 128

MLA_HEADS = 16
MLA_NOPE = 64
MLA_ROPE = 32
MLA_V = 64
MLA_Q_RANK = 384
MLA_KV_RANK = 256
ROPE_BASE = 10000.0
MLA_QK_PAD = 128

POOL_WINDOWS = (2, 4, 8, 16)
POOL_GROUP = D_MODEL // len(POOL_WINDOWS)
POOL_HALO = 8

LANES = 128
SUBLANES = 8
ROW_TILE = 256
WKV_CHUNK = 64
WKV_STEP_CHUNKS = 2
VMEM_LIMIT = 56 * 1024 * 1024


def _cparams(sem):
    return pltpu.CompilerParams(dimension_semantics=sem, vmem_limit_bytes=VMEM_LIMIT)


def _full(a):
    nd = a.ndim
    return pl.BlockSpec(a.shape, lambda *_: (0,) * nd)


def _sigmoid(x):
    return 1.0 / (1.0 + jnp.exp(-x))


def _rms(x, g):
    return x * lax.rsqrt(jnp.mean(x * x, axis=-1, keepdims=True) + NORM_EPS) * g


def _pre(s, mod_ref, slot, gpre_ref):
    shift = mod_ref[0, 3 * slot:3 * slot + 1, :]
    scale = mod_ref[0, 3 * slot + 1:3 * slot + 2, :]
    return _rms(s, gpre_ref[slot:slot + 1, :]) * (1.0 + scale) + shift


def _post(s, y, mod_ref, slot, gpost_ref, weight):
    gate = mod_ref[0, 3 * slot + 2:3 * slot + 3, :]
    return s + (weight * gate) * _rms(y, gpost_ref[slot:slot + 1, :])


def _bdot(a, b):
    return jnp.dot(a.astype(BF16), b.astype(BF16), preferred_element_type=F32)


def _row_spec(width=D_MODEL):
    return pl.BlockSpec((ROW_TILE, width), lambda i: (i, 0))


def _mod_spec(n_rows):
    nxt = (n_rows - CTX_LEN) // ROW_TILE
    return pl.BlockSpec((1, N_MOD, D_MODEL), lambda i: (jnp.where(i >= nxt, 0, 1), 0, 0))


def _halo_specs(n_rows):
    per = ROW_TILE // SUBLANES
    nblk = n_rows // SUBLANES
    prev = pl.BlockSpec((SUBLANES, D_MODEL), lambda i: (jnp.maximum(i * per - 1, 0), 0))
    nxt = pl.BlockSpec((SUBLANES, D_MODEL), lambda i: (jnp.minimum((i + 1) * per, nblk - 1), 0))
    return prev, nxt


def _stream_edges():
    i = pl.program_id(0)
    nt = pl.num_programs(0)
    nxt = nt - CTX_LEN // ROW_TILE
    first = jnp.logical_or(i == 0, i == nxt)
    last = jnp.logical_or(i == nxt - 1, i == nt - 1)
    return first, last


def _mod_body(c_ref, w_ref, b_ref, o_ref):
    c = c_ref[...]
    s = c * _sigmoid(c)
    o_ref[0] = _bdot(s, w_ref[0]) + b_ref[0]


def _modulation(c2, mod_w, mod_b):
    tn = D_MODEL
    out = pl.pallas_call(
        _mod_body,
        grid=(DEPTH, N_MOD * D_MODEL // tn),
        in_specs=[pl.BlockSpec((SUBLANES, D_MODEL), lambda l, j: (0, 0)),
                  pl.BlockSpec((1, D_MODEL, tn), lambda l, j: (l, 0, j)),
                  pl.BlockSpec((1, 1, tn), lambda l, j: (l, 0, j))],
        out_specs=pl.BlockSpec((1, SUBLANES, tn), lambda l, j: (l, 0, j)),
        out_shape=jax.ShapeDtypeStruct((DEPTH, SUBLANES, N_MOD * D_MODEL), F32),
        compiler_params=_cparams(("arbitrary", "arbitrary")),
        name="modulation",
    )(c2, mod_w, mod_b.reshape(DEPTH, 1, N_MOD * D_MODEL))
    return out[:, :2].reshape(DEPTH, 2, N_MOD, D_MODEL)


def _ffn_body(s_ref, mod_ref, gpre_ref, gpost_ref, wg_ref, wu_ref, wd_ref, o_ref, *, slot):
    s = s_ref[...]
    h = _pre(s, mod_ref, slot, gpre_ref).astype(BF16)
    g = jnp.dot(h, wg_ref[...], preferred_element_type=F32)
    u = jnp.dot(h, wu_ref[...], preferred_element_type=F32)
    act = (g * _sigmoid(g)) * u
    y = jnp.dot(act.astype(BF16), wd_ref[...], preferred_element_type=F32)
    o_ref[...] = _post(s, y, mod_ref, slot, gpost_ref, FFN_RES)


def _ffn(s, mod, gpre, gpost, wg, wu, wd, slot):
    n = s.shape[0]
    return pl.pallas_call(
        functools.partial(_ffn_body, slot=slot),
        grid=(n // ROW_TILE,),
        in_specs=[_row_spec(), _mod_spec(n), _full(gpre), _full(gpost),
                  _full(wg), _full(wu), _full(wd)],
        out_specs=_row_spec(),
        out_shape=jax.ShapeDtypeStruct(s.shape, F32),
        compiler_params=_cparams(("arbitrary",)),
        name="ffn",
    )(s, mod, gpre, gpost, wg, wu, wd)


def _rwkv_proj_body(*refs, vres):
    (s_ref, sp_ref, sn_ref, mod_ref, gpre_ref, mu_ref, wr_ref, wk_ref, wv_ref,
     w0_ref, w1_ref, w2_ref, a0_ref, a1_ref, a2_ref, g1_ref, g2_ref) = refs[:17]
    refs = refs[17:]
    if vres:
        v0_ref, v1_ref, v2_ref, vf_ref = refs[:4]
        refs = refs[4:]
    r_ref, k_ref, v_ref, g_ref, lw_ref, a_ref = refs

    first, last = _stream_edges()
    h = _pre(s_ref[...], mod_ref, 1, gpre_ref)
    hp = _pre(sp_ref[SUBLANES - 1:SUBLANES, :], mod_ref, 1, gpre_ref)
    hn = _pre(sn_ref[0:1, :], mod_ref, 1, gpre_ref)
    hp = jnp.where(first, 0.0, hp)
    hn = jnp.where(last, 0.0, hn)
    row = lax.broadcasted_iota(jnp.int32, h.shape, 0)
    prev = jnp.where(row == 0, hp, pltpu.roll(h, 1, 0))
    nxt = jnp.where(row == ROW_TILE - 1, hn, pltpu.roll(h, ROW_TILE - 1, 0))
    xx = 0.5 * (prev + nxt) - h

    def mix(n):
        return h + xx * mu_ref[n:n + 1, :]

    r_ref[...] = _bdot(mix(0), wr_ref[...])
    k_ref[...] = _bdot(mix(2), wk_ref[...])
    xvb = mix(3).astype(BF16)
    v = jnp.dot(xvb, wv_ref[...], preferred_element_type=F32)
    if vres:
        gate = _sigmoid(v0_ref[...] + _bdot(jnp.dot(xvb, v1_ref[...], preferred_element_type=F32),
                                            v2_ref[...]))
        v = v + (vf_ref[...] - v) * gate
    v_ref[...] = v
    g_ref[...] = _bdot(_sigmoid(_bdot(mix(5), g1_ref[...])), g2_ref[...])
    lw = w0_ref[...] + _bdot(jnp.tanh(_bdot(mix(1), w1_ref[...])), w2_ref[...])
    logw = -math.exp(-0.5) * _sigmoid(lw)
    aa = _sigmoid(a0_ref[...] + _bdot(_bdot(mix(4), a1_ref[...]), a2_ref[...]))
    for d in range(2):
        lw_ref[d] = logw[:, d * D_MODEL:(d + 1) * D_MODEL]
        a_ref[d] = aa[:, d * D_MODEL:(d + 1) * D_MODEL]


def _lora_pair(w1, w2, w0):
    lora = w1.shape[-1]
    w1c = jnp.concatenate([w1[0], w1[1]], axis=1)
    z = jnp.zeros((lora, D_MODEL), w2.dtype)
    w2c = jnp.concatenate([jnp.concatenate([w2[0], z], axis=1),
                           jnp.concatenate([z, w2[1]], axis=1)], axis=0)
    return w1c.astype(BF16), w2c.astype(BF16), w0.reshape(1, 2 * D_MODEL)


def _pad_to(a, axis, size):
    pad = [(0, 0)] * a.ndim
    pad[axis] = (0, size - a.shape[axis])
    return jnp.pad(a, pad)


def _rwkv_proj(s, mod, gpre, p, v_first):
    n = s.shape[0]
    vres = v_first is not None
    w1c, w2c, w0c = _lora_pair(p['w1'], p['w2'], p['w0'])
    a1c, a2c, a0c = _lora_pair(p['a1'], p['a2'], p['a0'])
    g1 = _pad_to(p['g1'], 1, RW_GATE_LORA_PAD).astype(BF16)
    g2 = _pad_to(p['g2'], 0, RW_GATE_LORA_PAD).astype(BF16)
    prev_spec, next_spec = _halo_specs(n)
    args = [s, s, s, mod, gpre, p['mu'], p['w_r'].astype(BF16), p['w_k'].astype(BF16),
            p['w_v'].astype(BF16), w0c, w1c, w2c, a0c, a1c, a2c, g1, g2]
    specs = [_row_spec(), prev_spec, next_spec, _mod_spec(n)] + [_full(a) for a in args[4:]]
    if vres:
        extra = [p['v0'].reshape(1, D_MODEL), _pad_to(p['v1'], 1, LORA_PAD).astype(BF16),
                 _pad_to(p['v2'], 0, LORA_PAD).astype(BF16)]
        args += extra + [v_first]
        specs += [_full(a) for a in extra] + [_row_spec()]
    row_out = jax.ShapeDtypeStruct((n, D_MODEL), F32)
    dir_out = jax.ShapeDtypeStruct((2, n, D_MODEL), F32)
    dir_spec = pl.BlockSpec((2, ROW_TILE, D_MODEL), lambda i: (0, i, 0))
    return pl.pallas_call(
        functools.partial(_rwkv_proj_body, vres=vres),
        grid=(n // ROW_TILE,),
        in_specs=specs,
        out_specs=[_row_spec()] * 4 + [dir_spec] * 2,
        out_shape=[row_out] * 4 + [dir_out] * 2,
        compiler_params=_cparams(("arbitrary",)),
        name="rwkv_proj",
    )(*args)


WKV_PAIR = LANES // RW_HEAD
WKV_ROWS = WKV_PAIR * WKV_CHUNK
WKV_GROUPS = D_MODEL // LANES
WKV_INV_LEVELS = tuple(range(1, WKV_CHUNK.bit_length() - 1))


def _wkv_masks(sgn):
    c = WKV_CHUNK
    ti = lax.broadcasted_iota(jnp.int32, (c, c), 0)
    si = lax.broadcasted_iota(jnp.int32, (c, c), 1)
    incl_c = ((ti - si) * sgn >= 0).astype(F32)
    rows = lax.broadcasted_iota(jnp.int32, (WKV_ROWS, LANES), 0)
    lanes = lax.broadcasted_iota(jnp.int32, (WKV_ROWS, LANES), 1)
    keep = (rows < c) == (lanes < RW_HEAD)
    tr = lax.broadcasted_iota(jnp.int32, (WKV_ROWS, WKV_ROWS), 0)
    sr = lax.broadcasted_iota(jnp.int32, (WKV_ROWS, WKV_ROWS), 1)
    same = (tr < c) == (sr < c)
    strict = jnp.logical_and(same, (tr - sr) * sgn > 0)
    incl = jnp.logical_and(same, (tr - sr) * sgn >= 0)
    eye = (tr == sr).astype(F32)
    blk = lambda x, b: lax.shift_right_logical(x, b)
    pairs = [blk(tr, 1) == blk(sr, 1)]
    for b in WKV_INV_LEVELS:
        pairs.append(jnp.logical_and(blk(tr, b + 1) == blk(sr, b + 1), blk(tr, b) != blk(sr, b)))
    return incl_c, keep, strict, incl, eye, pairs


def _dot_nt(x, y):
    return lax.dot_general(x.astype(BF16), y.astype(BF16), (((1,), (1,)), ((), ())),
                           preferred_element_type=F32)


def _pdot(x, y, prec):
    if prec is None:
        return _bdot(x, y)
    return jnp.dot(x, y, precision=prec, preferred_element_type=F32)


def _wkv_group(lw, cum, a, k, v, r, k_k, k_a, r_k, ln_w, ln_b, st_ref, o_ref, masks,
               prec_inv, prec_state):
    c = WKV_CHUNK
    _, keep, strict, incl, eye, pairs = masks
    head_a = lax.broadcasted_iota(jnp.int32, (c, LANES), 1) < RW_HEAD

    def head_sum(x):
        sa = jnp.sum(jnp.where(head_a, x, 0.0), axis=-1, keepdims=True)
        sb = jnp.sum(jnp.where(head_a, 0.0, x), axis=-1, keepdims=True)
        return jnp.where(head_a, sa, sb)

    kkf = k * k_k
    kap = kkf * lax.rsqrt(jnp.maximum(head_sum(kkf * kkf), 1e-24))
    kd = k * (1.0 + (a - 1.0) * k_a)
    b = kap * a

    tot = jnp.sum(lw, axis=0, keepdims=True)
    e_neg = jnp.exp(-cum)
    e_rem = jnp.exp(tot - cum)
    kt = kap * jnp.exp(cum - lw)
    rt = r * jnp.exp(cum)
    g_c = jnp.exp(tot)

    def stack(x):
        return jnp.where(keep, jnp.concatenate([x, x], axis=0), 0.0)

    kt_s, rt_s, v_s = stack(kt), stack(rt), stack(v)
    kh_s, bh_s = stack(kd * e_neg), stack(b * e_neg)
    kb_s, bb_s = stack(kd * e_rem), stack(b * e_rem)
    kbt, bbt = kb_s.T.astype(BF16), bb_s.T.astype(BF16)
    bonus = head_sum(r * kd * r_k) * v
    yield

    r2 = WKV_ROWS
    prod = _dot_nt(jnp.concatenate([kt_s, rt_s], axis=0), jnp.concatenate([kh_s, bh_s], axis=0))
    l_k = jnp.where(strict, prod[:r2, :r2], 0.0)
    l_b = jnp.where(strict, prod[:r2, r2:], 0.0)
    p_k = jnp.where(incl, prod[r2:, :r2], 0.0).astype(BF16)
    p_b = jnp.where(incl, prod[r2:, r2:], 0.0).astype(BF16)
    yield

    lkv = _bdot(l_k, v_s)
    t_inv = eye - jnp.where(pairs[0], l_b, 0.0)
    for lvl in range(1, len(pairs)):
        x = _pdot(jnp.where(pairs[lvl], l_b, 0.0), t_inv, prec_inv)
        yield
        t_inv = t_inv - _pdot(t_inv, x, prec_inv)
        yield

    wu = _bdot(t_inv, jnp.concatenate([kt_s, lkv], axis=1))
    yield
    wu = wu.astype(BF16)
    pbwu = jnp.dot(p_b, wu, preferred_element_type=F32)
    q_m = rt_s - pbwu[:, :LANES]
    y_0 = _bdot(p_k, v_s) - pbwu[:, LANES:]
    yield
    bbwu = jnp.dot(bbt, wu, preferred_element_type=F32)
    g_m = eye * g_c - bbwu[:, :LANES]
    h_m = _bdot(kbt, v_s) - bbwu[:, LANES:]
    yield

    a_state = st_ref[...]
    y_s = _pdot(q_m, a_state, prec_state) + y_0
    st_ref[...] = _pdot(g_m, a_state, prec_state) + h_m
    y = y_s[:c] + y_s[c:]

    mu = head_sum(y) * (1.0 / RW_HEAD)
    yc = y - mu
    var = head_sum(yc * yc) * (1.0 / RW_HEAD)
    yn = yc * lax.rsqrt(var + RW_GN_EPS) * ln_w + ln_b
    o_ref[...] = yn + bonus


def _wkv_body(lw_ref, a_ref, k_ref, v_ref, r_ref, kk_ref, ka_ref, rk_ref, lnw_ref, lnb_ref,
              o_ref, st_ref, *, prec_inv, prec_state):
    @pl.when(pl.program_id(1) == 0)
    def _():
        st_ref[...] = jnp.zeros_like(st_ref)

    fwd = pl.program_id(0) == 0
    masks = _wkv_masks(1 - 2 * pl.program_id(0))
    groups = []
    for i in range(WKV_STEP_CHUNKS):
        first = jnp.where(fwd, i, WKV_STEP_CHUNKS - 1 - i) * WKV_CHUNK
        rows = pl.ds(pl.multiple_of(first, WKV_CHUNK), WKV_CHUNK)
        cum = jnp.dot(masks[0], lw_ref[0, rows, :], precision=HIGHEST, preferred_element_type=F32)
        for g in range(WKV_GROUPS):
            sl = slice(g * LANES, (g + 1) * LANES)
            groups.append(_wkv_group(lw_ref[0, rows, sl], cum[:, sl], a_ref[0, rows, sl],
                                     k_ref[rows, sl], v_ref[rows, sl], r_ref[rows, sl],
                                     kk_ref[:, sl], ka_ref[:, sl], rk_ref[:, sl], lnw_ref[:, sl],
                                     lnb_ref[:, sl], st_ref.at[g], o_ref.at[0, rows, sl], masks,
                                     prec_inv, prec_state))
    while groups:
        groups = [gen for gen in groups if next(gen, "done") != "done"]


def _wkv(logw, a, k, v, r, p, prec_inv=None, prec_state=None):
    n = k.shape[0]
    blk = WKV_STEP_CHUNKS * WKV_CHUNK
    assert n % blk == 0 and CTX_LEN % blk == 0
    nb = n // blk
    nbc = CTX_LEN // blk
    nbx = nb - nbc

    def chunk(d, j):
        fwd = jnp.where(j < nbc, nbx + j, j - nbc)
        return jnp.where(d == 0, fwd, nb - 1 - j)

    dir_spec = pl.BlockSpec((1, blk, D_MODEL), lambda d, j: (d, chunk(d, j), 0))
    row_spec = pl.BlockSpec((blk, D_MODEL), lambda d, j: (chunk(d, j), 0))
    vecs = [p['k_k'].reshape(1, D_MODEL), p['k_a'].reshape(1, D_MODEL),
            p['r_k'].reshape(1, D_MODEL), p['ln_w'].reshape(1, D_MODEL),
            p['ln_b'].reshape(1, D_MODEL)]
    return pl.pallas_call(
        functools.partial(_wkv_body, prec_inv=prec_inv, prec_state=prec_state),
        grid=(2, nb),
        in_specs=[dir_spec, dir_spec, row_spec, row_spec, row_spec] + [_full(x) for x in vecs],
        out_specs=dir_spec,
        out_shape=jax.ShapeDtypeStruct((2, n, D_MODEL), F32),
        scratch_shapes=[pltpu.VMEM((WKV_GROUPS, WKV_ROWS, LANES), F32)],
        compiler_params=_cparams(("arbitrary", "arbitrary")),
        name="wkv_scan",
    )(logw, a, k, v, r, *vecs)


def _out_body(*refs, gated):
    if gated:
        o_ref, g_ref, w_ref, s_ref, mod_ref, gpost_ref, out_ref = refs
        y = (o_ref[0] + o_ref[1]) * g_ref[...]
    else:
        o_ref, w_ref, s_ref, mod_ref, gpost_ref, out_ref = refs
        y = o_ref[...]
    y = jnp.dot(y.astype(BF16), w_ref[...], preferred_element_type=F32)
    out_ref[...] = _post(s_ref[...], y, mod_ref, 1, gpost_ref, 1.0)


def _mixer_out(o, g, w_o, s, mod, gpost):
    n = s.shape[0]
    gated = g is not None
    if gated:
        args = [o, g, w_o, s, mod, gpost]
        specs = [pl.BlockSpec((2, ROW_TILE, D_MODEL), lambda i: (0, i, 0)), _row_spec(),
                 _full(w_o), _row_spec(), _mod_spec(n), _full(gpost)]
    else:
        args = [o, w_o, s, mod, gpost]
        specs = [_row_spec(), _full(w_o), _row_spec(), _mod_spec(n), _full(gpost)]
    return pl.pallas_call(
        functools.partial(_out_body, gated=gated),
        grid=(n // ROW_TILE,),
        in_specs=specs,
        out_specs=_row_spec(),
        out_shape=jax.ShapeDtypeStruct(s.shape, F32),
        compiler_params=_cparams(("arbitrary",)),
        name="mixer_out",
    )(*args)


def _mla_proj_body(s_ref, mod_ref, gpre_ref, wdq_ref, qn_ref, wq1_ref, wq2_ref, wdkv_ref, kvn_ref,
                   wkr_ref, wk_ref, wv_ref, place_ref, tq_ref, tk_ref, q_ref, k_ref, v_ref):
    h = _pre(s_ref[...], mod_ref, 1, gpre_ref).astype(BF16)
    cq = jnp.dot(h, wdq_ref[...], preferred_element_type=F32)
    qn = _rms(cq, qn_ref[...]).astype(BF16)
    q1 = jnp.dot(qn, wq1_ref[...], preferred_element_type=F32)
    q2 = jnp.dot(qn, wq2_ref[...], preferred_element_type=F32)
    ckv = jnp.dot(h, wdkv_ref[...], preferred_element_type=F32)
    kvn = _rms(ckv, kvn_ref[...]).astype(BF16)
    kr = jnp.dot(h, wkr_ref[...], preferred_element_type=F32)
    krr = kr * tk_ref[0] + pltpu.roll(kr, LANES - MLA_ROPE, 1) * tk_ref[1]
    kk = (jnp.dot(kvn, wk_ref[...], preferred_element_type=F32)
          + jnp.dot(krr.astype(BF16), place_ref[...], preferred_element_type=F32))
    k_ref[...] = kk.astype(BF16)
    v_ref[0] = jnp.dot(kvn, wv_ref[...], preferred_element_type=F32).T.astype(BF16)
    cos = tq_ref[0]
    sin = tq_ref[1]
    for hd in range(MLA_HEADS):
        sl = slice(hd * MLA_QK_PAD, (hd + 1) * MLA_QK_PAD)
        q_ref[:, sl] = (q1[:, sl] * cos + q2[:, sl] * sin).astype(BF16)


def _rot_cols(w):
    half = MLA_ROPE // 4
    parts = []
    for ax in range(2):
        blk = w[..., ax * 2 * half:(ax + 1) * 2 * half]
        parts += [-blk[..., half:], blk[..., :half]]
    return jnp.concatenate(parts, axis=-1)


def _mla_tables(n):
    t = n - CTX_LEN
    pos = jnp.arange(t, dtype=jnp.int32)
    rowp = (pos // GRID_W).astype(F32)
    colp = (pos % GRID_W).astype(F32)
    axis_dim = MLA_ROPE // 2
    inv = ROPE_BASE ** (-jnp.arange(0, axis_dim, 2, dtype=F32) / axis_dim)
    ar = rowp[:, None] * inv
    ac = colp[:, None] * inv
    cosf = jnp.concatenate([jnp.cos(ar), jnp.cos(ar), jnp.cos(ac), jnp.cos(ac)], axis=-1)
    sinf = jnp.concatenate([jnp.sin(ar), jnp.sin(ar), jnp.sin(ac), jnp.sin(ac)], axis=-1)
    cosf = jnp.concatenate([cosf, jnp.ones((CTX_LEN, MLA_ROPE), F32)], axis=0)
    sinf = jnp.concatenate([sinf, jnp.zeros((CTX_LEN, MLA_ROPE), F32)], axis=0)
    scale = (MLA_NOPE + MLA_ROPE) ** -0.5 * math.log2(math.e)
    zq = jnp.zeros((n, MLA_QK_PAD - MLA_NOPE - MLA_ROPE), F32)
    tq = jnp.stack([jnp.concatenate([jnp.ones((n, MLA_NOPE), F32), cosf, zq], axis=-1),
                    jnp.concatenate([jnp.zeros((n, MLA_NOPE), F32), sinf, zq], axis=-1)]) * scale
    zk = jnp.zeros((n, LANES - MLA_ROPE), F32)
    tk = jnp.stack([jnp.concatenate([cosf, zk], axis=-1), jnp.concatenate([sinf, zk], axis=-1)])
    return tq, tk


def _mla_proj(s, mod, gpre, w_dq, q_norm, w_uq, w_dkv, kv_norm, w_ukv, key_tile):
    n = s.shape[0]
    per = key_tile // ROW_TILE
    hq = MLA_NOPE + MLA_ROPE
    wq = w_uq.reshape(MLA_Q_RANK, MLA_HEADS, hq)
    zq = jnp.zeros((MLA_Q_RANK, MLA_HEADS, MLA_QK_PAD - hq), w_uq.dtype)
    wq1 = jnp.concatenate([wq, zq], axis=-1).reshape(MLA_Q_RANK, MLA_HEADS * MLA_QK_PAD)
    wq2 = jnp.concatenate([jnp.zeros_like(wq[..., :MLA_NOPE]), _rot_cols(wq[..., MLA_NOPE:]), zq],
                          axis=-1).reshape(MLA_Q_RANK, MLA_HEADS * MLA_QK_PAD)
    wkv = w_ukv.reshape(MLA_KV_RANK, MLA_HEADS, MLA_NOPE + MLA_V)
    wk = jnp.concatenate([wkv[..., :MLA_NOPE],
                          jnp.zeros((MLA_KV_RANK, MLA_HEADS, MLA_QK_PAD - MLA_NOPE), w_ukv.dtype)],
                         axis=-1).reshape(MLA_KV_RANK, MLA_HEADS * MLA_QK_PAD)
    wv = wkv[..., MLA_NOPE:].reshape(MLA_KV_RANK, MLA_HEADS * MLA_V)
    w_rope = w_dkv[:, MLA_KV_RANK:]
    wkr = jnp.concatenate([w_rope, _rot_cols(w_rope),
                           jnp.zeros((D_MODEL, LANES - 2 * MLA_ROPE), w_dkv.dtype)], axis=-1)
    src = jnp.arange(LANES)[:, None]
    dst = jnp.arange(MLA_HEADS * MLA_QK_PAD)[None, :]
    place = jnp.logical_and(src < MLA_ROPE, dst % MLA_QK_PAD == MLA_NOPE + src).astype(BF16)
    tq, tk = _mla_tables(n)
    weights = [w_dq.astype(BF16), q_norm.reshape(1, MLA_Q_RANK), wq1.astype(BF16), wq2.astype(BF16),
               w_dkv[:, :MLA_KV_RANK].astype(BF16), kv_norm.reshape(1, MLA_KV_RANK),
               wkr.astype(BF16), wk.astype(BF16), wv.astype(BF16), place]
    tab_spec = pl.BlockSpec((2, ROW_TILE, LANES), lambda i: (0, i, 0))
    return pl.pallas_call(
        _mla_proj_body,
        grid=(n // ROW_TILE,),
        in_specs=[_row_spec(), _mod_spec(n), _full(gpre)] + [_full(w) for w in weights]
                 + [tab_spec, tab_spec],
        out_specs=[_row_spec(MLA_HEADS * MLA_QK_PAD), _row_spec(MLA_HEADS * MLA_QK_PAD),
                   pl.BlockSpec((1, MLA_HEADS * MLA_V, ROW_TILE), lambda i: (i // per, 0, i % per))],
        out_shape=[jax.ShapeDtypeStruct((n, MLA_HEADS * MLA_QK_PAD), BF16),
                   jax.ShapeDtypeStruct((n, MLA_HEADS * MLA_QK_PAD), BF16),
                   jax.ShapeDtypeStruct((n // key_tile, MLA_HEADS * MLA_V, key_tile), BF16)],
        compiler_params=_cparams(("arbitrary",)),
        name="mla_proj",
    )(s, mod, gpre, *weights, tq, tk)


MLA_PAIR = LANES // MLA_V


ATTN_KEY_BLOCK = 16
ATTN_SAFE_LOG2 = 90.0


def _attn_body(q_ref, k_ref, vt_ref, o_ref, m_ref, l_ref, acc_ref, s_ref, p_ref, kb_ref, *, tk, nk):
    tq = q_ref.shape[0]
    l_ref[...] = jnp.zeros_like(l_ref)
    acc_ref[...] = jnp.zeros_like(acc_ref)

    def row_norm2_max(x):
        xf = x.astype(F32)
        return jnp.max(jnp.sum(xf * xf, axis=-1, keepdims=True), axis=0, keepdims=True)

    @pl.when(pl.program_id(1) == 0)
    def _():
        for hd in range(MLA_PAIR):
            sl = slice(hd * MLA_QK_PAD, (hd + 1) * MLA_QK_PAD)

            def key_tile(t, best):
                rows = pl.ds(pl.multiple_of(t * tk, tk), tk)
                return jnp.maximum(best, row_norm2_max(k_ref[rows, sl]))

            kb_ref[hd] = lax.fori_loop(0, nk, key_tile, jnp.zeros((1, 1), F32))

    bound2 = jnp.zeros((1, 1), F32)
    for hd in range(MLA_PAIR):
        sl = slice(hd * MLA_QK_PAD, (hd + 1) * MLA_QK_PAD)
        bound2 = jnp.maximum(bound2, row_norm2_max(q_ref[:, sl]) * kb_ref[hd])
    small = bound2[0, 0] <= ATTN_SAFE_LOG2 ** 2

    @pl.when(small)
    def _():
        def tile(t, carry):
            rows = pl.ds(pl.multiple_of(t * tk, tk), tk)
            for hd in range(MLA_PAIR):
                sl = slice(hd * MLA_QK_PAD, (hd + 1) * MLA_QK_PAD)
                s = lax.dot_general(k_ref[rows, sl], q_ref[:, sl], (((1,), (1,)), ((), ())),
                                    preferred_element_type=F32)
                p = jnp.exp2(s)
                l_ref[hd] += jnp.sum(p.reshape(tk // SUBLANES, SUBLANES, tq), axis=0)
                vt = vt_ref[t, hd * MLA_V:(hd + 1) * MLA_V, :]
                acc_ref[hd] += jnp.dot(vt, p.astype(BF16), preferred_element_type=F32)
            return carry

        lax.fori_loop(0, nk, tile, 0)

    @pl.when(jnp.logical_not(small))
    def _():
        _attn_running_max(q_ref, k_ref, vt_ref, m_ref, l_ref, acc_ref, s_ref, p_ref, tk=tk, nk=nk)

    outs = [acc_ref[hd] / jnp.sum(l_ref[hd], axis=0, keepdims=True) for hd in range(MLA_PAIR)]
    o_ref[...] = jnp.concatenate(outs, axis=0).T.astype(o_ref.dtype)


def _attn_running_max(q_ref, k_ref, vt_ref, m_ref, l_ref, acc_ref, s_ref, p_ref, *, tk, nk):
    tq = q_ref.shape[0]
    m_ref[...] = jnp.full_like(m_ref, -jnp.inf)

    def scores(t, slot):
        rows = pl.ds(pl.multiple_of(t * tk, tk), tk)
        for hd in range(MLA_PAIR):
            sl = slice(hd * MLA_QK_PAD, (hd + 1) * MLA_QK_PAD)
            s_ref[slot, hd] = lax.dot_general(k_ref[rows, sl], q_ref[:, sl], (((1,), (1,)), ((), ())),
                                              preferred_element_type=F32)

    def softmax_pv(t, slot):
        for hd in range(MLA_PAIR):
            mx = s_ref[slot, hd, 0:SUBLANES, :]
            for j in range(1, tk // SUBLANES):
                mx = jnp.maximum(mx, s_ref[slot, hd, j * SUBLANES:(j + 1) * SUBLANES, :])
            m_old = m_ref[hd]
            m_new = jnp.maximum(m_old, jnp.max(mx, axis=0, keepdims=True))
            m_ref[hd] = m_new
            alpha = jnp.exp2(m_old - m_new)
            m_rows = jnp.broadcast_to(m_new, (ATTN_KEY_BLOCK, tq))
            lsum = alpha * l_ref[hd]
            for b in range(tk // ATTN_KEY_BLOCK):
                r = slice(b * ATTN_KEY_BLOCK, (b + 1) * ATTN_KEY_BLOCK)
                p = jnp.exp2(s_ref[slot, hd, r, :] - m_rows)
                lsum = lsum + (p[:SUBLANES] + p[SUBLANES:])
                p_ref[hd, r, :] = p.astype(BF16)
            l_ref[hd] = lsum
            vt = vt_ref[t, hd * MLA_V:(hd + 1) * MLA_V, :]
            acc_ref[hd] = alpha * acc_ref[hd] + jnp.dot(vt, p_ref[hd], preferred_element_type=F32)

    scores(0, 0)

    def pair(j, carry):
        t = 2 * j
        scores(t + 1, 1)
        softmax_pv(t, 0)
        scores(t + 2, 0)
        softmax_pv(t + 1, 1)
        return carry

    lax.fori_loop(0, (nk - 1) // 2, pair, 0)
    if nk % 2:
        softmax_pv(nk - 1, 0)
    else:
        scores(nk - 1, 1)
        softmax_pv(nk - 2, 0)
        softmax_pv(nk - 1, 1)


def _attention(q, k, vt, q_row0, q_rows, k_row0, k_rows, tq, tk):
    vtile = vt.shape[2]
    assert q_row0 % tq == 0 and q_rows % tq == 0 and k_row0 % k_rows == 0 and k_rows % tk == 0
    q0 = q_row0 // tq
    k0 = k_row0 // k_rows
    pair_w = MLA_PAIR * MLA_QK_PAD
    if k_rows % vtile == 0:
        assert tk == vtile
        vt_spec = pl.BlockSpec((k_rows // vtile, LANES, vtile), lambda p, i: (k_row0 // k_rows, p, 0))
    else:
        assert vtile % k_rows == 0 and tk == k_rows
        vt_spec = pl.BlockSpec((1, LANES, k_rows),
                               lambda p, i: (k_row0 // vtile, p, (k_row0 % vtile) // k_rows))
    return pl.pallas_call(
        functools.partial(_attn_body, tk=tk, nk=k_rows // tk),
        grid=(MLA_HEADS // MLA_PAIR, q_rows // tq),
        in_specs=[pl.BlockSpec((tq, pair_w), lambda p, i: (i + q0, p)),
                  pl.BlockSpec((k_rows, pair_w), lambda p, i: (k0, p)),
                  vt_spec],
        out_specs=pl.BlockSpec((tq, LANES), lambda p, i: (i, p)),
        out_shape=jax.ShapeDtypeStruct((q_rows, MLA_HEADS * MLA_V), BF16),
        scratch_shapes=[pltpu.VMEM((MLA_PAIR, 1, tq), F32), pltpu.VMEM((MLA_PAIR, SUBLANES, tq), F32),
                        pltpu.VMEM((MLA_PAIR, MLA_V, tq), F32), pltpu.VMEM((2, MLA_PAIR, tk, tq), F32),
                        pltpu.VMEM((MLA_PAIR, tk, tq), BF16), pltpu.VMEM((MLA_PAIR, 1, 1), F32)],
        compiler_params=_cparams(("arbitrary", "arbitrary")),
        name="mla_attention",
    )(q, k, vt)


def _largest_divisor(n, candidates):
    for c in candidates:
        if n % c == 0:
            return c
    raise ValueError(f"no tile in {candidates} divides {n}")


def _pool_body(s_ref, sp_ref, sn_ref, mod_ref, gpre_ref, gpost_ref, w_ref, b_ref, sc_ref, o_ref):
    i = pl.program_id(0)
    nt = pl.num_programs(0)
    nct = CTX_LEN // ROW_TILE
    first, last = _stream_edges()
    s = s_ref[...]
    h = _pre(s, mod_ref, 1, gpre_ref)
    hp = jnp.where(first, 0.0, _pre(sp_ref[...], mod_ref, 1, gpre_ref))
    hn = jnp.where(last, 0.0, _pre(sn_ref[...], mod_ref, 1, gpre_ref))
    ext = jnp.concatenate([hp, h, hn], axis=0)
    er = ROW_TILE + 2 * POOL_HALO
    in_ctx = i >= nt - nct
    t0 = jnp.where(in_ctx, i - (nt - nct), i) * ROW_TILE
    t_len = jnp.where(in_ctx, nct, nt - nct) * ROW_TILE
    t = t0 + lax.broadcasted_iota(jnp.int32, (ROW_TILE, 1), 0)

    outs = []
    run = ext
    width = 1
    for gi, win in enumerate(POOL_WINDOWS):
        while width < win:
            run = run + pltpu.roll(run, width, 0)
            width *= 2
        ahead = win // 2 - 1
        grp = run[:, :POOL_GROUP]
        if ahead:
            grp = pltpu.roll(grp, er - ahead, 0)
        wsum = grp[POOL_HALO:POOL_HALO + ROW_TILE]
        cnt = (jnp.minimum(t + win // 2, t_len) - jnp.maximum(t - win // 2, 0)).astype(F32)
        diff = wsum / cnt - h[:, gi * POOL_GROUP:(gi + 1) * POOL_GROUP]
        outs.append(_bdot(diff, w_ref[gi]) + b_ref[gi:gi + 1, :])
        run = run[:, POOL_GROUP:]
    y = jnp.concatenate(outs, axis=-1) * sc_ref[...]
    o_ref[...] = _post(s, y, mod_ref, 1, gpost_ref, 1.0)


def _pool(s, mod, gpre, gpost, w, b, scale):
    n = s.shape[0]
    prev_spec, next_spec = _halo_specs(n)
    sc = scale.reshape(1, D_MODEL)
    wb = w.astype(BF16)
    return pl.pallas_call(
        _pool_body,
        grid=(n // ROW_TILE,),
        in_specs=[_row_spec(), prev_spec, next_spec, _mod_spec(n), _full(gpre), _full(gpost),
                  _full(wb), _full(b), _full(sc)],
        out_specs=_row_spec(),
        out_shape=jax.ShapeDtypeStruct(s.shape, F32),
        compiler_params=_cparams(("arbitrary",)),
        name="pool_mixer",
    )(s, s, s, mod, gpre, gpost, wb, b, sc)


def kernel(x, c, ctx, c_ctx, mod_w, mod_b, norm_pre, norm_post, ffn_w_gate, ffn_w_up, ffn_w_down,
           rw_mu, rw_w_r, rw_w_k, rw_w_v, rw_w_o, rw_w0, rw_w1, rw_w2, rw_a0, rw_a1, rw_a2,
           rw_v0, rw_v1, rw_v2, rw_g1, rw_g2, rw_k_k, rw_k_a, rw_r_k, rw_ln_w, rw_ln_b,
           mla_w_dq, mla_q_norm, mla_w_uq, mla_w_dkv, mla_kv_norm, mla_w_ukv, mla_w_o,
           pool_w, pool_b, pool_scale):
    batch, t, d = x.shape
    assert batch == 1 and d == D_MODEL and ctx.shape == (1, CTX_LEN, D_MODEL)
    assert t % ROW_TILE == 0 and CTX_LEN % ROW_TILE == 0 and t % GRID_W == 0
    n = t + CTX_LEN
    s = jnp.concatenate([x[0], ctx[0]], axis=0)
    c2 = jnp.concatenate([c_ctx[None], c, jnp.zeros((SUBLANES - 2, D_MODEL), F32)], axis=0)
    mod_all = _modulation(c2, mod_w, mod_b)
    v_first = None
    for i in range(DEPTH):
        kind, j = i % 3, i // 3
        mod = mod_all[i]
        gpre, gpost = norm_pre[i], norm_post[i]
        s = _ffn(s, mod, gpre, gpost, ffn_w_gate[i, 0].astype(BF16), ffn_w_up[i, 0].astype(BF16),
                 ffn_w_down[i, 0].astype(BF16), 0)
        if kind == 0:
            p = {'mu': rw_mu[j], 'w_r': rw_w_r[j], 'w_k': rw_w_k[j], 'w_v': rw_w_v[j],
                 'w0': rw_w0[j], 'w1': rw_w1[j], 'w2': rw_w2[j],
                 'a0': rw_a0[j], 'a1': rw_a1[j], 'a2': rw_a2[j],
                 'g1': rw_g1[j], 'g2': rw_g2[j], 'k_k': rw_k_k[j], 'k_a': rw_k_a[j],
                 'r_k': rw_r_k[j], 'ln_w': rw_ln_w[j], 'ln_b': rw_ln_b[j]}
            if j > 0:
                p['v0'], p['v1'], p['v2'] = rw_v0[j - 1], rw_v1[j - 1], rw_v2[j - 1]
            r, k, v, g, logw, a = _rwkv_proj(s, mod, gpre, p, v_first if j > 0 else None)
            if j == 0:
                v_first = v
            o = _wkv(logw, a, k, v, r, p)
            s = _mixer_out(o, g, rw_w_o[j].astype(BF16), s, mod, gpost)
        elif kind == 1:
            tq = _largest_divisor(t, (512, 256))
            tk = _largest_divisor(n, (1280, 256))
            q, k, vt = _mla_proj(s, mod, gpre, mla_w_dq[j], mla_q_norm[j], mla_w_uq[j],
                                 mla_w_dkv[j], mla_kv_norm[j], mla_w_ukv[j], tk)
            o_x = _attention(q, k, vt, 0, t, 0, n, tq, tk)
            o_c = _attention(q, k, vt, t, CTX_LEN, t, CTX_LEN, CTX_LEN, CTX_LEN)
            o = jnp.concatenate([o_x, o_c], axis=0)
            s = _mixer_out(o, None, mla_w_o[j].astype(BF16), s, mod, gpost)
        else:
            s = _pool(s, mod, gpre, gpost, pool_w[j], pool_b[j], pool_scale[j])
        s = _ffn(s, mod, gpre, gpost, ffn_w_gate[i, 1].astype(BF16), ffn_w_up[i, 1].astype(BF16),
                 ffn_w_down[i, 1].astype(BF16), 2)
    return s[:t][None]
```

```python
import functools
import math

import jax
import jax.numpy as jnp
from jax import lax
from jax.experimental import pallas as pl
from jax.experimental.pallas import tpu as pltpu

F32 = jnp.float32
BF16 = jnp.bfloat16

D_MODEL = 1024
DEPTH = 4
CTX_LEN = 256
GRID_W = 64
N_MOD = 9
FFN_RES = 0.5
NORM_EPS = 1e-6

RW_HEAD = 64
RW_GN_EPS = 64e-5
RW_GATE_LORA_PAD = 256
LORA_PAD = 128

MLA_HEADS = 16
MLA_NOPE = 64
MLA_ROPE = 32
MLA_V = 64
MLA_Q_RANK = 384
MLA_KV_RANK = 256
ROPE_BASE = 10000.0
MLA_QK_PAD = 128

POOL_WINDOWS = (2, 4, 8, 16)
POOL_GROUP = D_MODEL // len(POOL_WINDOWS)
POOL_HALO = 8

LANES = 128
SUBLANES = 8
ROW_TILE = 256
WKV_CHUNK = 64
WKV_STEP_CHUNKS = 2
VMEM_LIMIT = 56 * 1024 * 1024


def _cparams(sem):
    return pltpu.CompilerParams(dimension_semantics=sem, vmem_limit_bytes=VMEM_LIMIT)


def _full(a):
    nd = a.ndim
    return pl.BlockSpec(a.shape, lambda *_: (0,) * nd)


def _sigmoid(x):
    return 1.0 / (1.0 + jnp.exp(-x))


def _rms(x, g):
    return x * lax.rsqrt(jnp.mean(x * x, axis=-1, keepdims=True) + NORM_EPS) * g


def _pre(s, mod_ref, slot, gpre_ref):
    shift = mod_ref[0, 3 * slot:3 * slot + 1, :]
    scale = mod_ref[0, 3 * slot + 1:3 * slot + 2, :]
    return _rms(s, gpre_ref[slot:slot + 1, :]) * (1.0 + scale) + shift


def _post(s, y, mod_ref, slot, gpost_ref, weight):
    gate = mod_ref[0, 3 * slot + 2:3 * slot + 3, :]
    return s + (weight * gate) * _rms(y, gpost_ref[slot:slot + 1, :])


def _bdot(a, b):
    return jnp.dot(a.astype(BF16), b.astype(BF16), preferred_element_type=F32)


def _row_spec(width=D_MODEL):
    return pl.BlockSpec((ROW_TILE, width), lambda i: (i, 0))


def _mod_spec(n_rows):
    nxt = (n_rows - CTX_LEN) // ROW_TILE
    return pl.BlockSpec((1, N_MOD, D_MODEL), lambda i: (jnp.where(i >= nxt, 0, 1), 0, 0))


def _halo_specs(n_rows):
    per = ROW_TILE // SUBLANES
    nblk = n_rows // SUBLANES
    prev = pl.BlockSpec((SUBLANES, D_MODEL), lambda i: (jnp.maximum(i * per - 1, 0), 0))
    nxt = pl.BlockSpec((SUBLANES, D_MODEL), lambda i: (jnp.minimum((i + 1) * per, nblk - 1), 0))
    return prev, nxt


def _stream_edges():
    i = pl.program_id(0)
    nt = pl.num_programs(0)
    nxt = nt - CTX_LEN // ROW_TILE
    first = jnp.logical_or(i == 0, i == nxt)
    last = jnp.logical_or(i == nxt - 1, i == nt - 1)
    return first, last


def _mod_body(c_ref, w_ref, b_ref, o_ref):
    c = c_ref[...]
    s = c * _sigmoid(c)
    o_ref[0] = _bdot(s, w_ref[0]) + b_ref[0]


def _modulation(c2, mod_w, mod_b):
    tn = D_MODEL
    out = pl.pallas_call(
        _mod_body,
        grid=(DEPTH, N_MOD * D_MODEL // tn),
        in_specs=[pl.BlockSpec((SUBLANES, D_MODEL), lambda l, j: (0, 0)),
                  pl.BlockSpec((1, D_MODEL, tn), lambda l, j: (l, 0, j)),
                  pl.BlockSpec((1, 1, tn), lambda l, j: (l, 0, j))],
        out_specs=pl.BlockSpec((1, SUBLANES, tn), lambda l, j: (l, 0, j)),
        out_shape=jax.ShapeDtypeStruct((DEPTH, SUBLANES, N_MOD * D_MODEL), F32),
        compiler_params=_cparams(("arbitrary", "arbitrary")),
        name="modulation",
    )(c2, mod_w, mod_b.reshape(DEPTH, 1, N_MOD * D_MODEL))
    return out[:, :2].reshape(DEPTH, 2, N_MOD, D_MODEL)


def _ffn_body(s_ref, mod_ref, gpre_ref, gpost_ref, wg_ref, wu_ref, wd_ref, o_ref, *, slot):
    s = s_ref[...]
    h = _pre(s, mod_ref, slot, gpre_ref).astype(BF16)
    g = jnp.dot(h, wg_ref[...], preferred_element_type=F32)
    u = jnp.dot(h, wu_ref[...], preferred_element_type=F32)
    act = (g * _sigmoid(g)) * u
    y = jnp.dot(act.astype(BF16), wd_ref[...], preferred_element_type=F32)
    o_ref[...] = _post(s, y, mod_ref, slot, gpost_ref, FFN_RES)


def _ffn(s, mod, gpre, gpost, wg, wu, wd, layer, which, slot, out_rows):
    n = s.shape[0]

    def weight_spec(w):
        return pl.BlockSpec((None, None) + w.shape[2:], lambda i: (layer, which, 0, 0))

    return pl.pallas_call(
        functools.partial(_ffn_body, slot=slot),
        grid=(out_rows // ROW_TILE,),
        in_specs=[_row_spec(), _mod_spec(n), _full(gpre), _full(gpost),
                  weight_spec(wg), weight_spec(wu), weight_spec(wd)],
        out_specs=_row_spec(),
        out_shape=jax.ShapeDtypeStruct((out_rows, D_MODEL), F32),
        compiler_params=_cparams(("arbitrary",)),
        name="ffn",
    )(s, mod, gpre, gpost, wg, wu, wd)


def _rwkv_proj_body(*refs, vres):
    (s_ref, sp_ref, sn_ref, mod_ref, gpre_ref, mu_ref, wr_ref, wk_ref, wv_ref,
     w0_ref, w1_ref, w2_ref, a0_ref, a1_ref, a2_ref, g1_ref, g2_ref) = refs[:17]
    refs = refs[17:]
    if vres:
        v0_ref, v1_ref, v2_ref, vf_ref = refs[:4]
        refs = refs[4:]
    r_ref, k_ref, v_ref, g_ref, lw_ref, a_ref = refs

    first, last = _stream_edges()
    h = _pre(s_ref[...], mod_ref, 1, gpre_ref)
    hp = _pre(sp_ref[SUBLANES - 1:SUBLANES, :], mod_ref, 1, gpre_ref)
    hn = _pre(sn_ref[0:1, :], mod_ref, 1, gpre_ref)
    hp = jnp.where(first, 0.0, hp)
    hn = jnp.where(last, 0.0, hn)
    row = lax.broadcasted_iota(jnp.int32, h.shape, 0)
    prev = jnp.where(row == 0, hp, pltpu.roll(h, 1, 0))
    nxt = jnp.where(row == ROW_TILE - 1, hn, pltpu.roll(h, ROW_TILE - 1, 0))
    xx = 0.5 * (prev + nxt) - h

    def mix(n):
        return h + xx * mu_ref[n:n + 1, :]

    r_ref[...] = _bdot(mix(0), wr_ref[...])
    k_ref[...] = _bdot(mix(2), wk_ref[...])
    xvb = mix(3).astype(BF16)
    v = jnp.dot(xvb, wv_ref[...], preferred_element_type=F32)
    if vres:
        gate = _sigmoid(v0_ref[...] + _bdot(jnp.dot(xvb, v1_ref[...], preferred_element_type=F32),
                                            v2_ref[...]))
        v = v + (vf_ref[...] - v) * gate
    v_ref[...] = v
    g_ref[...] = _bdot(_sigmoid(_bdot(mix(5), g1_ref[...])), g2_ref[...])
    lw = w0_ref[...] + _bdot(jnp.tanh(_bdot(mix(1), w1_ref[...])), w2_ref[...])
    logw = -math.exp(-0.5) * _sigmoid(lw)
    aa = _sigmoid(a0_ref[...] + _bdot(_bdot(mix(4), a1_ref[...]), a2_ref[...]))
    for d in range(2):
        lw_ref[d] = logw[:, d * D_MODEL:(d + 1) * D_MODEL]
        a_ref[d] = aa[:, d * D_MODEL:(d + 1) * D_MODEL]


def _lora_pair(w1, w2, w0):
    lora = w1.shape[-1]
    w1c = jnp.concatenate([w1[0], w1[1]], axis=1)
    z = jnp.zeros((lora, D_MODEL), w2.dtype)
    w2c = jnp.concatenate([jnp.concatenate([w2[0], z], axis=1),
                           jnp.concatenate([z, w2[1]], axis=1)], axis=0)
    return w1c.astype(BF16), w2c.astype(BF16), w0.reshape(1, 2 * D_MODEL)


def _pad_to(a, axis, size):
    pad = [(0, 0)] * a.ndim
    pad[axis] = (0, size - a.shape[axis])
    return jnp.pad(a, pad)


def _rwkv_proj(s, mod, gpre, p, v_first):
    n = s.shape[0]
    vres = v_first is not None
    w1c, w2c, w0c = _lora_pair(p['w1'], p['w2'], p['w0'])
    a1c, a2c, a0c = _lora_pair(p['a1'], p['a2'], p['a0'])
    g1 = _pad_to(p['g1'], 1, RW_GATE_LORA_PAD).astype(BF16)
    g2 = _pad_to(p['g2'], 0, RW_GATE_LORA_PAD).astype(BF16)
    prev_spec, next_spec = _halo_specs(n)
    args = [s, s, s, mod, gpre, p['mu'], p['w_r'].astype(BF16), p['w_k'].astype(BF16),
            p['w_v'].astype(BF16), w0c, w1c, w2c, a0c, a1c, a2c, g1, g2]
    specs = [_row_spec(), prev_spec, next_spec, _mod_spec(n)] + [_full(a) for a in args[4:]]
    if vres:
        extra = [p['v0'].reshape(1, D_MODEL), _pad_to(p['v1'], 1, LORA_PAD).astype(BF16),
                 _pad_to(p['v2'], 0, LORA_PAD).astype(BF16)]
        args += extra + [v_first]
        specs += [_full(a) for a in extra] + [_row_spec()]
    row_out = jax.ShapeDtypeStruct((n, D_MODEL), F32)
    dir_out = jax.ShapeDtypeStruct((2, n, D_MODEL), F32)
    dir_spec = pl.BlockSpec((2, ROW_TILE, D_MODEL), lambda i: (0, i, 0))
    return pl.pallas_call(
        functools.partial(_rwkv_proj_body, vres=vres),
        grid=(n // ROW_TILE,),
        in_specs=specs,
        out_specs=[_row_spec()] * 4 + [dir_spec] * 2,
        out_shape=[row_out] * 4 + [dir_out] * 2,
        compiler_params=_cparams(("arbitrary",)),
        name="rwkv_proj",
    )(*args)


WKV_PAIR = LANES // RW_HEAD
WKV_ROWS = WKV_PAIR * WKV_CHUNK
WKV_GROUPS = D_MODEL // LANES
WKV_INV_LEVELS = tuple(range(1, WKV_CHUNK.bit_length() - 1))


def _wkv_masks(sgn):
    c = WKV_CHUNK
    rows = lax.broadcasted_iota(jnp.int32, (WKV_ROWS, LANES), 0)
    lanes = lax.broadcasted_iota(jnp.int32, (WKV_ROWS, LANES), 1)
    keep = (rows < c) == (lanes < RW_HEAD)
    tr = lax.broadcasted_iota(jnp.int32, (WKV_ROWS, WKV_ROWS), 0)
    sr = lax.broadcasted_iota(jnp.int32, (WKV_ROWS, WKV_ROWS), 1)
    same = (tr < c) == (sr < c)
    strict = jnp.logical_and(same, (tr - sr) * sgn > 0)
    incl = jnp.logical_and(same, (tr - sr) * sgn >= 0)
    eye = (tr == sr).astype(F32)
    blk = lambda x, b: lax.shift_right_logical(x, b)
    pairs = [blk(tr, 1) == blk(sr, 1)]
    for b in WKV_INV_LEVELS:
        pairs.append(jnp.logical_and(blk(tr, b + 1) == blk(sr, b + 1), blk(tr, b) != blk(sr, b)))
    return keep, strict, incl, eye, pairs


def _cumsum_rows(x):
    row = lax.broadcasted_iota(jnp.int32, x.shape, 0)
    k = 1
    while k < x.shape[0]:
        x = x + jnp.where(row >= k, pltpu.roll(x, k, 0), 0.0)
        k *= 2
    return x


def _dot_nt(x, y):
    return lax.dot_general(x.astype(BF16), y.astype(BF16), (((1,), (1,)), ((), ())),
                           preferred_element_type=F32)


def _pdot(x, y, prec):
    if prec is None:
        return _bdot(x, y)
    return jnp.dot(x, y, precision=prec, preferred_element_type=F32)


def _wkv_group(lw, cum, a, k, v, r, k_k, k_a, r_k, ln_w, ln_b, st_ref, o_ref, masks,
               prec_inv, prec_state):
    c = WKV_CHUNK
    keep, strict, incl, eye, pairs = masks
    head_a = lax.broadcasted_iota(jnp.int32, (c, LANES), 1) < RW_HEAD

    def head_sum(x):
        sa = jnp.sum(jnp.where(head_a, x, 0.0), axis=-1, keepdims=True)
        sb = jnp.sum(jnp.where(head_a, 0.0, x), axis=-1, keepdims=True)
        return jnp.where(head_a, sa, sb)

    kkf = k * k_k
    kap = kkf * lax.rsqrt(jnp.maximum(head_sum(kkf * kkf), 1e-24))
    kd = k * (1.0 + (a - 1.0) * k_a)
    b = kap * a

    tot = jnp.sum(lw, axis=0, keepdims=True)
    e_neg = jnp.exp(-cum)
    e_rem = jnp.exp(tot - cum)
    kt = kap * jnp.exp(cum - lw)
    rt = r * jnp.exp(cum)
    g_c = jnp.exp(tot)

    def stack(x):
        return jnp.where(keep, jnp.concatenate([x, x], axis=0), 0.0)

    kt_s, rt_s, v_s = stack(kt), stack(rt), stack(v)
    kh_s, bh_s = stack(kd * e_neg), stack(b * e_neg)
    kb_s, bb_s = stack(kd * e_rem), stack(b * e_rem)
    kbt, bbt = kb_s.T.astype(BF16), bb_s.T.astype(BF16)
    bonus = head_sum(r * kd * r_k) * v
    yield

    r2 = WKV_ROWS
    prod = _dot_nt(jnp.concatenate([kt_s, rt_s], axis=0), jnp.concatenate([kh_s, bh_s], axis=0))
    l_k = jnp.where(strict, prod[:r2, :r2], 0.0)
    l_b = jnp.where(strict, prod[:r2, r2:], 0.0)
    p_k = jnp.where(incl, prod[r2:, :r2], 0.0).astype(BF16)
    p_b = jnp.where(incl, prod[r2:, r2:], 0.0).astype(BF16)
    yield

    lkv = _bdot(l_k, v_s)
    t_inv = eye - jnp.where(pairs[0], l_b, 0.0)
    for lvl in range(1, len(pairs)):
        x = _pdot(jnp.where(pairs[lvl], l_b, 0.0), t_inv, prec_inv)
        yield
        t_inv = t_inv - _pdot(t_inv, x, prec_inv)
        yield

    wu = _bdot(t_inv, jnp.concatenate([kt_s, lkv], axis=1))
    yield
    wu = wu.astype(BF16)
    pbwu = jnp.dot(p_b, wu, preferred_element_type=F32)
    q_m = rt_s - pbwu[:, :LANES]
    y_0 = _bdot(p_k, v_s) - pbwu[:, LANES:]
    yield
    bbwu = jnp.dot(bbt, wu, preferred_element_type=F32)
    g_m = eye * g_c - bbwu[:, :LANES]
    h_m = _bdot(kbt, v_s) - bbwu[:, LANES:]
    yield

    a_state = st_ref[...]
    y_s = _pdot(q_m, a_state, prec_state) + y_0
    st_ref[...] = _pdot(g_m, a_state, prec_state) + h_m
    y = y_s[:c] + y_s[c:]

    mu = head_sum(y) * (1.0 / RW_HEAD)
    yc = y - mu
    var = head_sum(yc * yc) * (1.0 / RW_HEAD)
    yn = yc * lax.rsqrt(var + RW_GN_EPS) * ln_w + ln_b
    o_ref[...] = yn + bonus


def _wkv_body(lw_ref, a_ref, k_ref, v_ref, r_ref, kk_ref, ka_ref, rk_ref, lnw_ref, lnb_ref,
              o_ref, st_ref, *, prec_inv, prec_state):
    @pl.when(pl.program_id(1) == 0)
    def _():
        st_ref[...] = jnp.zeros_like(st_ref)

    fwd = pl.program_id(0) == 0
    masks = _wkv_masks(1 - 2 * pl.program_id(0))
    groups = []
    for i in range(WKV_STEP_CHUNKS):
        first = jnp.where(fwd, i, WKV_STEP_CHUNKS - 1 - i) * WKV_CHUNK
        rows = pl.ds(pl.multiple_of(first, WKV_CHUNK), WKV_CHUNK)
        lw_all = lw_ref[0, rows, :]
        pre = _cumsum_rows(lw_all)
        cum = jnp.where(fwd, pre, pre[WKV_CHUNK - 1:WKV_CHUNK, :] - pre + lw_all)
        for g in range(WKV_GROUPS):
            sl = slice(g * LANES, (g + 1) * LANES)
            groups.append(_wkv_group(lw_ref[0, rows, sl], cum[:, sl], a_ref[0, rows, sl],
                                     k_ref[rows, sl], v_ref[rows, sl], r_ref[rows, sl],
                                     kk_ref[:, sl], ka_ref[:, sl], rk_ref[:, sl], lnw_ref[:, sl],
                                     lnb_ref[:, sl], st_ref.at[g], o_ref.at[0, rows, sl], masks,
                                     prec_inv, prec_state))
    while groups:
        groups = [gen for gen in groups if next(gen, "done") != "done"]


def _wkv(logw, a, k, v, r, p, prec_inv=None, prec_state=None):
    n = k.shape[0]
    blk = WKV_STEP_CHUNKS * WKV_CHUNK
    assert n % blk == 0 and CTX_LEN % blk == 0
    nb = n // blk
    nbc = CTX_LEN // blk
    nbx = nb - nbc

    def chunk(d, j):
        fwd = jnp.where(j < nbc, nbx + j, j - nbc)
        return jnp.where(d == 0, fwd, nb - 1 - j)

    dir_spec = pl.BlockSpec((1, blk, D_MODEL), lambda d, j: (d, chunk(d, j), 0))
    row_spec = pl.BlockSpec((blk, D_MODEL), lambda d, j: (chunk(d, j), 0))
    vecs = [p['k_k'].reshape(1, D_MODEL), p['k_a'].reshape(1, D_MODEL),
            p['r_k'].reshape(1, D_MODEL), p['ln_w'].reshape(1, D_MODEL),
            p['ln_b'].reshape(1, D_MODEL)]
    return pl.pallas_call(
        functools.partial(_wkv_body, prec_inv=prec_inv, prec_state=prec_state),
        grid=(2, nb),
        in_specs=[dir_spec, dir_spec, row_spec, row_spec, row_spec] + [_full(x) for x in vecs],
        out_specs=dir_spec,
        out_shape=jax.ShapeDtypeStruct((2, n, D_MODEL), F32),
        scratch_shapes=[pltpu.VMEM((WKV_GROUPS, WKV_ROWS, LANES), F32)],
        compiler_params=_cparams(("arbitrary", "arbitrary")),
        name="wkv_scan",
    )(logw, a, k, v, r, *vecs)


def _out_body(*refs, gated):
    if gated:
        o_ref, g_ref, w_ref, s_ref, mod_ref, gpost_ref, out_ref = refs
        y = (o_ref[0] + o_ref[1]) * g_ref[...]
    else:
        o_ref, w_ref, s_ref, mod_ref, gpost_ref, out_ref = refs
        y = o_ref[...]
    y = jnp.dot(y.astype(BF16), w_ref[...], preferred_element_type=F32)
    out_ref[...] = _post(s_ref[...], y, mod_ref, 1, gpost_ref, 1.0)


def _mixer_out(o, g, w_o, s, mod, gpost):
    n = s.shape[0]
    gated = g is not None
    if gated:
        args = [o, g, w_o, s, mod, gpost]
        specs = [pl.BlockSpec((2, ROW_TILE, D_MODEL), lambda i: (0, i, 0)), _row_spec(),
                 _full(w_o), _row_spec(), _mod_spec(n), _full(gpost)]
    else:
        args = [o, w_o, s, mod, gpost]
        specs = [_row_spec(), _full(w_o), _row_spec(), _mod_spec(n), _full(gpost)]
    return pl.pallas_call(
        functools.partial(_out_body, gated=gated),
        grid=(n // ROW_TILE,),
        in_specs=specs,
        out_specs=_row_spec(),
        out_shape=jax.ShapeDtypeStruct(s.shape, F32),
        compiler_params=_cparams(("arbitrary",)),
        name="mixer_out",
    )(*args)


def _mla_proj_body(s_ref, mod_ref, gpre_ref, wdq_ref, qn_ref, wq1_ref, wq2_ref, wdkv_ref, kvn_ref,
                   wkr_ref, wk_ref, wv_ref, place_ref, tq_ref, tk_ref, q_ref, k_ref, v_ref):
    h = _pre(s_ref[...], mod_ref, 1, gpre_ref).astype(BF16)
    cq = jnp.dot(h, wdq_ref[...], preferred_element_type=F32)
    qn = _rms(cq, qn_ref[...]).astype(BF16)
    q1 = jnp.dot(qn, wq1_ref[...], preferred_element_type=F32)
    q2 = jnp.dot(qn, wq2_ref[...], preferred_element_type=F32)
    ckv = jnp.dot(h, wdkv_ref[...], preferred_element_type=F32)
    kvn = _rms(ckv, kvn_ref[...]).astype(BF16)
    kr = jnp.dot(h, wkr_ref[...], preferred_element_type=F32)
    krr = kr * tk_ref[0] + pltpu.roll(kr, LANES - MLA_ROPE, 1) * tk_ref[1]
    kk = (jnp.dot(kvn, wk_ref[...], preferred_element_type=F32)
          + jnp.dot(krr.astype(BF16), place_ref[...], preferred_element_type=F32))
    k_ref[...] = kk.astype(BF16)
    v_ref[0] = jnp.dot(kvn, wv_ref[...], preferred_element_type=F32).T.astype(BF16)
    cos = tq_ref[0]
    sin = tq_ref[1]
    for hd in range(MLA_HEADS):
        sl = slice(hd * MLA_QK_PAD, (hd + 1) * MLA_QK_PAD)
        q_ref[:, sl] = (q1[:, sl] * cos + q2[:, sl] * sin).astype(BF16)


def _rot_cols(w):
    half = MLA_ROPE // 4
    parts = []
    for ax in range(2):
        blk = w[..., ax * 2 * half:(ax + 1) * 2 * half]
        parts += [-blk[..., half:], blk[..., :half]]
    return jnp.concatenate(parts, axis=-1)


def _mla_tables(n):
    t = n - CTX_LEN
    pos = jnp.arange(t, dtype=jnp.int32)
    rowp = (pos // GRID_W).astype(F32)
    colp = (pos % GRID_W).astype(F32)
    axis_dim = MLA_ROPE // 2
    inv = ROPE_BASE ** (-jnp.arange(0, axis_dim, 2, dtype=F32) / axis_dim)
    ar = rowp[:, None] * inv
    ac = colp[:, None] * inv
    cosf = jnp.concatenate([jnp.cos(ar), jnp.cos(ar), jnp.cos(ac), jnp.cos(ac)], axis=-1)
    sinf = jnp.concatenate([jnp.sin(ar), jnp.sin(ar), jnp.sin(ac), jnp.sin(ac)], axis=-1)
    cosf = jnp.concatenate([cosf, jnp.ones((CTX_LEN, MLA_ROPE), F32)], axis=0)
    sinf = jnp.concatenate([sinf, jnp.zeros((CTX_LEN, MLA_ROPE), F32)], axis=0)
    scale = (MLA_NOPE + MLA_ROPE) ** -0.5 * math.log2(math.e)
    zq = jnp.zeros((n, MLA_QK_PAD - MLA_NOPE - MLA_ROPE), F32)
    tq = jnp.stack([jnp.concatenate([jnp.ones((n, MLA_NOPE), F32), cosf, zq], axis=-1),
                    jnp.concatenate([jnp.zeros((n, MLA_NOPE), F32), sinf, zq], axis=-1)]) * scale
    zk = jnp.zeros((n, LANES - MLA_ROPE), F32)
    tk = jnp.stack([jnp.concatenate([cosf, zk], axis=-1), jnp.concatenate([sinf, zk], axis=-1)])
    return tq, tk


def _mla_proj(s, mod, gpre, w_dq, q_norm, w_uq, w_dkv, kv_norm, w_ukv, key_tile):
    n = s.shape[0]
    per = key_tile // ROW_TILE
    hq = MLA_NOPE + MLA_ROPE
    wq = w_uq.reshape(MLA_Q_RANK, MLA_HEADS, hq)
    zq = jnp.zeros((MLA_Q_RANK, MLA_HEADS, MLA_QK_PAD - hq), w_uq.dtype)
    wq1 = jnp.concatenate([wq, zq], axis=-1).reshape(MLA_Q_RANK, MLA_HEADS * MLA_QK_PAD)
    wq2 = jnp.concatenate([jnp.zeros_like(wq[..., :MLA_NOPE]), _rot_cols(wq[..., MLA_NOPE:]), zq],
                          axis=-1).reshape(MLA_Q_RANK, MLA_HEADS * MLA_QK_PAD)
    wkv = w_ukv.reshape(MLA_KV_RANK, MLA_HEADS, MLA_NOPE + MLA_V)
    wk = jnp.concatenate([wkv[..., :MLA_NOPE],
                          jnp.zeros((MLA_KV_RANK, MLA_HEADS, MLA_QK_PAD - MLA_NOPE), w_ukv.dtype)],
                         axis=-1).reshape(MLA_KV_RANK, MLA_HEADS * MLA_QK_PAD)
    wv = wkv[..., MLA_NOPE:].reshape(MLA_KV_RANK, MLA_HEADS * MLA_V)
    w_rope = w_dkv[:, MLA_KV_RANK:]
    wkr = jnp.concatenate([w_rope, _rot_cols(w_rope),
                           jnp.zeros((D_MODEL, LANES - 2 * MLA_ROPE), w_dkv.dtype)], axis=-1)
    src = jnp.arange(LANES)[:, None]
    dst = jnp.arange(MLA_HEADS * MLA_QK_PAD)[None, :]
    place = jnp.logical_and(src < MLA_ROPE, dst % MLA_QK_PAD == MLA_NOPE + src).astype(BF16)
    tq, tk = _mla_tables(n)
    weights = [w_dq.astype(BF16), q_norm.reshape(1, MLA_Q_RANK), wq1.astype(BF16), wq2.astype(BF16),
               w_dkv[:, :MLA_KV_RANK].astype(BF16), kv_norm.reshape(1, MLA_KV_RANK),
               wkr.astype(BF16), wk.astype(BF16), wv.astype(BF16), place]
    tab_spec = pl.BlockSpec((2, ROW_TILE, LANES), lambda i: (0, i, 0))
    return pl.pallas_call(
        _mla_proj_body,
        grid=(n // ROW_TILE,),
        in_specs=[_row_spec(), _mod_spec(n), _full(gpre)] + [_full(w) for w in weights]
                 + [tab_spec, tab_spec],
        out_specs=[_row_spec(MLA_HEADS * MLA_QK_PAD), _row_spec(MLA_HEADS * MLA_QK_PAD),
                   pl.BlockSpec((1, MLA_HEADS * MLA_V, ROW_TILE), lambda i: (i // per, 0, i % per))],
        out_shape=[jax.ShapeDtypeStruct((n, MLA_HEADS * MLA_QK_PAD), BF16),
                   jax.ShapeDtypeStruct((n, MLA_HEADS * MLA_QK_PAD), BF16),
                   jax.ShapeDtypeStruct((n // key_tile, MLA_HEADS * MLA_V, key_tile), BF16)],
        compiler_params=_cparams(("arbitrary",)),
        name="mla_proj",
    )(s, mod, gpre, *weights, tq, tk)


MLA_PAIR = LANES // MLA_V


ATTN_KEY_BLOCK = 16
ATTN_SAFE_LOG2 = 90.0


def _attn_body(q_ref, k_ref, vt_ref, o_ref, m_ref, l_ref, acc_ref, s_ref, p_ref, kb_ref, *, tk, nk):
    tq = q_ref.shape[0]
    l_ref[...] = jnp.zeros_like(l_ref)
    acc_ref[...] = jnp.zeros_like(acc_ref)

    def row_norm2_max(x):
        xf = x.astype(F32)
        return jnp.max(jnp.sum(xf * xf, axis=-1, keepdims=True), axis=0, keepdims=True)

    @pl.when(pl.program_id(1) == 0)
    def _():
        for hd in range(MLA_PAIR):
            sl = slice(hd * MLA_QK_PAD, (hd + 1) * MLA_QK_PAD)

            def key_tile(t, best):
                rows = pl.ds(pl.multiple_of(t * tk, tk), tk)
                return jnp.maximum(best, row_norm2_max(k_ref[rows, sl]))

            kb_ref[hd] = lax.fori_loop(0, nk, key_tile, jnp.zeros((1, 1), F32))

    bound2 = jnp.zeros((1, 1), F32)
    for hd in range(MLA_PAIR):
        sl = slice(hd * MLA_QK_PAD, (hd + 1) * MLA_QK_PAD)
        bound2 = jnp.maximum(bound2, row_norm2_max(q_ref[:, sl]) * kb_ref[hd])
    small = bound2[0, 0] <= ATTN_SAFE_LOG2 ** 2

    @pl.when(small)
    def _():
        def tile(t, carry):
            rows = pl.ds(pl.multiple_of(t * tk, tk), tk)
            for hd in range(MLA_PAIR):
                sl = slice(hd * MLA_QK_PAD, (hd + 1) * MLA_QK_PAD)
                s = lax.dot_general(k_ref[rows, sl], q_ref[:, sl], (((1,), (1,)), ((), ())),
                                    preferred_element_type=F32)
                p = jnp.exp2(s)
                l_ref[hd] += jnp.sum(p.reshape(tk // SUBLANES, SUBLANES, tq), axis=0)
                vt = vt_ref[t, hd * MLA_V:(hd + 1) * MLA_V, :]
                acc_ref[hd] += jnp.dot(vt, p.astype(BF16), preferred_element_type=F32)
            return carry

        lax.fori_loop(0, nk, tile, 0)

    @pl.when(jnp.logical_not(small))
    def _():
        _attn_running_max(q_ref, k_ref, vt_ref, m_ref, l_ref, acc_ref, s_ref, p_ref, tk=tk, nk=nk)

    outs = [acc_ref[hd] / jnp.sum(l_ref[hd], axis=0, keepdims=True) for hd in range(MLA_PAIR)]
    o_ref[...] = jnp.concatenate(outs, axis=0).T.astype(o_ref.dtype)


def _attn_running_max(q_ref, k_ref, vt_ref, m_ref, l_ref, acc_ref, s_ref, p_ref, *, tk, nk):
    tq = q_ref.shape[0]
    m_ref[...] = jnp.full_like(m_ref, -jnp.inf)

    def tile(t, carry):
        rows = pl.ds(pl.multiple_of(t * tk, tk), tk)
        for hd in range(MLA_PAIR):
            sl = slice(hd * MLA_QK_PAD, (hd + 1) * MLA_QK_PAD)
            s_ref[hd] = lax.dot_general(k_ref[rows, sl], q_ref[:, sl], (((1,), (1,)), ((), ())),
                                        preferred_element_type=F32)
        for hd in range(MLA_PAIR):
            mx = s_ref[hd, 0:SUBLANES, :]
            for j in range(1, tk // SUBLANES):
                mx = jnp.maximum(mx, s_ref[hd, j * SUBLANES:(j + 1) * SUBLANES, :])
            m_old = m_ref[hd]
            m_new = jnp.maximum(m_old, jnp.max(mx, axis=0, keepdims=True))
            m_ref[hd] = m_new
            alpha = jnp.exp2(m_old - m_new)
            m_rows = jnp.broadcast_to(m_new, (ATTN_KEY_BLOCK, tq))
            lsum = alpha * l_ref[hd]
            for b in range(tk // ATTN_KEY_BLOCK):
                r = slice(b * ATTN_KEY_BLOCK, (b + 1) * ATTN_KEY_BLOCK)
                p = jnp.exp2(s_ref[hd, r, :] - m_rows)
                lsum = lsum + (p[:SUBLANES] + p[SUBLANES:])
                p_ref[hd, r, :] = p.astype(BF16)
            l_ref[hd] = lsum
            vt = vt_ref[t, hd * MLA_V:(hd + 1) * MLA_V, :]
            acc_ref[hd] = alpha * acc_ref[hd] + jnp.dot(vt, p_ref[hd], preferred_element_type=F32)
        return carry

    lax.fori_loop(0, nk, tile, 0)


def _attention(q, k, vt, q_row0, q_rows, k_row0, k_rows, tq, tk):
    vtile = vt.shape[2]
    assert q_row0 % tq == 0 and q_rows % tq == 0 and k_row0 % k_rows == 0 and k_rows % tk == 0
    q0 = q_row0 // tq
    k0 = k_row0 // k_rows
    pair_w = MLA_PAIR * MLA_QK_PAD
    if k_rows % vtile == 0:
        assert tk == vtile
        vt_spec = pl.BlockSpec((k_rows // vtile, LANES, vtile), lambda p, i: (k_row0 // k_rows, p, 0))
    else:
        assert vtile % k_rows == 0 and tk == k_rows
        vt_spec = pl.BlockSpec((1, LANES, k_rows),
                               lambda p, i: (k_row0 // vtile, p, (k_row0 % vtile) // k_rows))
    return pl.pallas_call(
        functools.partial(_attn_body, tk=tk, nk=k_rows // tk),
        grid=(MLA_HEADS // MLA_PAIR, q_rows // tq),
        in_specs=[pl.BlockSpec((tq, pair_w), lambda p, i: (i + q0, p)),
                  pl.BlockSpec((k_rows, pair_w), lambda p, i: (k0, p)),
                  vt_spec],
        out_specs=pl.BlockSpec((tq, LANES), lambda p, i: (i, p)),
        out_shape=jax.ShapeDtypeStruct((q_rows, MLA_HEADS * MLA_V), BF16),
        scratch_shapes=[pltpu.VMEM((MLA_PAIR, 1, tq), F32), pltpu.VMEM((MLA_PAIR, SUBLANES, tq), F32),
                        pltpu.VMEM((MLA_PAIR, MLA_V, tq), F32), pltpu.VMEM((MLA_PAIR, tk, tq), F32),
                        pltpu.VMEM((MLA_PAIR, tk, tq), BF16), pltpu.VMEM((MLA_PAIR, 1, 1), F32)],
        compiler_params=_cparams(("arbitrary", "arbitrary")),
        name="mla_attention",
    )(q, k, vt)


def _largest_divisor(n, candidates):
    for c in candidates:
        if n % c == 0:
            return c
    raise ValueError(f"no tile in {candidates} divides {n}")


def _pool_body(s_ref, sp_ref, sn_ref, mod_ref, gpre_ref, gpost_ref, w_ref, b_ref, sc_ref, o_ref):
    i = pl.program_id(0)
    nt = pl.num_programs(0)
    nct = CTX_LEN // ROW_TILE
    first, last = _stream_edges()
    s = s_ref[...]
    h = _pre(s, mod_ref, 1, gpre_ref)
    hp = jnp.where(first, 0.0, _pre(sp_ref[...], mod_ref, 1, gpre_ref))
    hn = jnp.where(last, 0.0, _pre(sn_ref[...], mod_ref, 1, gpre_ref))
    ext = jnp.concatenate([hp, h, hn], axis=0)
    er = ROW_TILE + 2 * POOL_HALO
    in_ctx = i >= nt - nct
    t0 = jnp.where(in_ctx, i - (nt - nct), i) * ROW_TILE
    t_len = jnp.where(in_ctx, nct, nt - nct) * ROW_TILE
    t = t0 + lax.broadcasted_iota(jnp.int32, (ROW_TILE, 1), 0)

    outs = []
    run = ext
    width = 1
    for gi, win in enumerate(POOL_WINDOWS):
        while width < win:
            run = run + pltpu.roll(run, width, 0)
            width *= 2
        ahead = win // 2 - 1
        grp = run[:, :POOL_GROUP]
        if ahead:
            grp = pltpu.roll(grp, er - ahead, 0)
        wsum = grp[POOL_HALO:POOL_HALO + ROW_TILE]
        cnt = (jnp.minimum(t + win // 2, t_len) - jnp.maximum(t - win // 2, 0)).astype(F32)
        diff = wsum / cnt - h[:, gi * POOL_GROUP:(gi + 1) * POOL_GROUP]
        outs.append(_bdot(diff, w_ref[gi]) + b_ref[gi:gi + 1, :])
        run = run[:, POOL_GROUP:]
    y = jnp.concatenate(outs, axis=-1) * sc_ref[...]
    o_ref[...] = _post(s, y, mod_ref, 1, gpost_ref, 1.0)


def _pool(s, mod, gpre, gpost, w, b, scale):
    n = s.shape[0]
    prev_spec, next_spec = _halo_specs(n)
    sc = scale.reshape(1, D_MODEL)
    wb = w.astype(BF16)
    return pl.pallas_call(
        _pool_body,
        grid=(n // ROW_TILE,),
        in_specs=[_row_spec(), prev_spec, next_spec, _mod_spec(n), _full(gpre), _full(gpost),
                  _full(wb), _full(b), _full(sc)],
        out_specs=_row_spec(),
        out_shape=jax.ShapeDtypeStruct(s.shape, F32),
        compiler_params=_cparams(("arbitrary",)),
        name="pool_mixer",
    )(s, s, s, mod, gpre, gpost, wb, b, sc)


def kernel(x, c, ctx, c_ctx, mod_w, mod_b, norm_pre, norm_post, ffn_w_gate, ffn_w_up, ffn_w_down,
           rw_mu, rw_w_r, rw_w_k, rw_w_v, rw_w_o, rw_w0, rw_w1, rw_w2, rw_a0, rw_a1, rw_a2,
           rw_v0, rw_v1, rw_v2, rw_g1, rw_g2, rw_k_k, rw_k_a, rw_r_k, rw_ln_w, rw_ln_b,
           mla_w_dq, mla_q_norm, mla_w_uq, mla_w_dkv, mla_kv_norm, mla_w_ukv, mla_w_o,
           pool_w, pool_b, pool_scale):
    batch, t, d = x.shape
    assert batch == 1 and d == D_MODEL and ctx.shape == (1, CTX_LEN, D_MODEL)
    assert t % ROW_TILE == 0 and CTX_LEN % ROW_TILE == 0 and t % GRID_W == 0
    n = t + CTX_LEN
    s = jnp.concatenate([x[0], ctx[0]], axis=0)
    c2 = jnp.concatenate([c_ctx[None], c, jnp.zeros((SUBLANES - 2, D_MODEL), F32)], axis=0)
    mod_all = _modulation(c2, mod_w, mod_b)
    wg_all, wu_all, wd_all = (w.astype(BF16) for w in (ffn_w_gate, ffn_w_up, ffn_w_down))
    v_first = None
    for i in range(DEPTH):
        kind, j = i % 3, i // 3
        mod = mod_all[i]
        gpre, gpost = norm_pre[i], norm_post[i]
        s = _ffn(s, mod, gpre, gpost, wg_all, wu_all, wd_all, i, 0, 0, n)
        if kind == 0:
            p = {'mu': rw_mu[j], 'w_r': rw_w_r[j], 'w_k': rw_w_k[j], 'w_v': rw_w_v[j],
                 'w0': rw_w0[j], 'w1': rw_w1[j], 'w2': rw_w2[j],
                 'a0': rw_a0[j], 'a1': rw_a1[j], 'a2': rw_a2[j],
                 'g1': rw_g1[j], 'g2': rw_g2[j], 'k_k': rw_k_k[j], 'k_a': rw_k_a[j],
                 'r_k': rw_r_k[j], 'ln_w': rw_ln_w[j], 'ln_b': rw_ln_b[j]}
            if j > 0:
                p['v0'], p['v1'], p['v2'] = rw_v0[j - 1], rw_v1[j - 1], rw_v2[j - 1]
            r, k, v, g, logw, a = _rwkv_proj(s, mod, gpre, p, v_first if j > 0 else None)
            if j == 0:
                v_first = v
            o = _wkv(logw, a, k, v, r, p)
            s = _mixer_out(o, g, rw_w_o[j].astype(BF16), s, mod, gpost)
        elif kind == 1:
            tq = _largest_divisor(t, (1024, 512, 256))
            tk = _largest_divisor(n, (1280, 256))
            q, k, vt = _mla_proj(s, mod, gpre, mla_w_dq[j], mla_q_norm[j], mla_w_uq[j],
                                 mla_w_dkv[j], mla_kv_norm[j], mla_w_ukv[j], tk)
            o_x = _attention(q, k, vt, 0, t, 0, n, tq, tk)
            o_c = _attention(q, k, vt, t, CTX_LEN, t, CTX_LEN, CTX_LEN, CTX_LEN)
            o = jnp.concatenate([o_x, o_c], axis=0)
            s = _mixer_out(o, None, mla_w_o[j].astype(BF16), s, mod, gpost)
        else:
            s = _pool(s, mod, gpre, gpost, pool_w[j], pool_b[j], pool_scale[j])
        s = _ffn(s, mod, gpre, gpost, wg_all, wu_all, wd_all, i, 1, 2, t if i == DEPTH - 1 else n)
    return s[None]
```

```python
import functools
import math

import jax
import jax.numpy as jnp
from jax import lax
from jax.experimental import pallas as pl
from jax.experimental.pallas import tpu as pltpu

F32 = jnp.float32
BF16 = jnp.bfloat16

D_MODEL = 1024
DEPTH = 4
CTX_LEN = 256
GRID_W = 64
N_MOD = 9
FFN_RES = 0.5
NORM_EPS = 1e-6

RW_HEAD = 64
RW_GN_EPS = 64e-5
RW_GATE_LORA_PAD = 256
LORA_PAD = 128

MLA_HEADS = 16
MLA_NOPE = 64
MLA_ROPE = 32
MLA_V = 64
MLA_Q_RANK = 384
MLA_KV_RANK = 256
ROPE_BASE = 10000.0
MLA_QK_PAD = 128

POOL_WINDOWS = (2, 4, 8, 16)
POOL_GROUP = D_MODEL // len(POOL_WINDOWS)
POOL_HALO = 8

LANES = 128
SUBLANES = 8
ROW_TILE = 256
WKV_CHUNK = 64
WKV_STEP_CHUNKS = 4
WKV_STAGE_LAG = 2
VMEM_LIMIT = 56 * 1024 * 1024


def _cparams(sem):
    return pltpu.CompilerParams(dimension_semantics=sem, vmem_limit_bytes=VMEM_LIMIT)


def _full(a):
    nd = a.ndim
    return pl.BlockSpec(a.shape, lambda *_: (0,) * nd)


def _sigmoid(x):
    return 1.0 / (1.0 + jnp.exp(-x))


def _rms(x, g):
    return x * lax.rsqrt(jnp.mean(x * x, axis=-1, keepdims=True) + NORM_EPS) * g


def _pre(s, mod_ref, slot, gpre_ref):
    shift = mod_ref[0, 3 * slot:3 * slot + 1, :]
    scale = mod_ref[0, 3 * slot + 1:3 * slot + 2, :]
    return _rms(s, gpre_ref[slot:slot + 1, :]) * (1.0 + scale) + shift


def _post(s, y, mod_ref, slot, gpost_ref, weight):
    gate = mod_ref[0, 3 * slot + 2:3 * slot + 3, :]
    return s + (weight * gate) * _rms(y, gpost_ref[slot:slot + 1, :])


def _bdot(a, b):
    return jnp.dot(a.astype(BF16), b.astype(BF16), preferred_element_type=F32)


def _row_spec(width=D_MODEL):
    return pl.BlockSpec((ROW_TILE, width), lambda i: (i, 0))


def _mod_spec(n_rows):
    nxt = (n_rows - CTX_LEN) // ROW_TILE
    return pl.BlockSpec((1, N_MOD, D_MODEL), lambda i: (jnp.where(i >= nxt, 0, 1), 0, 0))


def _halo_specs(n_rows):
    per = ROW_TILE // SUBLANES
    nblk = n_rows // SUBLANES
    prev = pl.BlockSpec((SUBLANES, D_MODEL), lambda i: (jnp.maximum(i * per - 1, 0), 0))
    nxt = pl.BlockSpec((SUBLANES, D_MODEL), lambda i: (jnp.minimum((i + 1) * per, nblk - 1), 0))
    return prev, nxt


def _stream_edges():
    i = pl.program_id(0)
    nt = pl.num_programs(0)
    nxt = nt - CTX_LEN // ROW_TILE
    first = jnp.logical_or(i == 0, i == nxt)
    last = jnp.logical_or(i == nxt - 1, i == nt - 1)
    return first, last


def _mod_body(c_ref, w_ref, b_ref, o_ref):
    c = c_ref[...]
    s = c * _sigmoid(c)
    o_ref[0] = _bdot(s, w_ref[0]) + b_ref[0]


def _modulation(c2, mod_w, mod_b):
    tn = D_MODEL
    out = pl.pallas_call(
        _mod_body,
        grid=(DEPTH, N_MOD * D_MODEL // tn),
        in_specs=[pl.BlockSpec((SUBLANES, D_MODEL), lambda l, j: (0, 0)),
                  pl.BlockSpec((1, D_MODEL, tn), lambda l, j: (l, 0, j)),
                  pl.BlockSpec((1, 1, tn), lambda l, j: (l, 0, j))],
        out_specs=pl.BlockSpec((1, SUBLANES, tn), lambda l, j: (l, 0, j)),
        out_shape=jax.ShapeDtypeStruct((DEPTH, SUBLANES, N_MOD * D_MODEL), F32),
        compiler_params=_cparams(("arbitrary", "arbitrary")),
        name="modulation",
    )(c2, mod_w, mod_b.reshape(DEPTH, 1, N_MOD * D_MODEL))
    return out[:, :2].reshape(DEPTH, 2, N_MOD, D_MODEL)


def _ffn_body(s_ref, mod_ref, gpre_ref, gpost_ref, wg_ref, wu_ref, wd_ref, o_ref, *, slot):
    s = s_ref[...]
    h = _pre(s, mod_ref, slot, gpre_ref).astype(BF16)
    g = jnp.dot(h, wg_ref[...], preferred_element_type=F32)
    u = jnp.dot(h, wu_ref[...], preferred_element_type=F32)
    act = (g * _sigmoid(g)) * u
    y = jnp.dot(act.astype(BF16), wd_ref[...], preferred_element_type=F32)
    o_ref[...] = _post(s, y, mod_ref, slot, gpost_ref, FFN_RES)


def _ffn(s, mod, gpre, gpost, wg, wu, wd, layer, which, slot, out_rows):
    n = s.shape[0]

    def weight_spec(w):
        return pl.BlockSpec((None, None) + w.shape[2:], lambda i: (layer, which, 0, 0))

    return pl.pallas_call(
        functools.partial(_ffn_body, slot=slot),
        grid=(out_rows // ROW_TILE,),
        in_specs=[_row_spec(), _mod_spec(n), _full(gpre), _full(gpost),
                  weight_spec(wg), weight_spec(wu), weight_spec(wd)],
        out_specs=_row_spec(),
        out_shape=jax.ShapeDtypeStruct((out_rows, D_MODEL), F32),
        compiler_params=_cparams(("arbitrary",)),
        name="ffn",
    )(s, mod, gpre, gpost, wg, wu, wd)


def _rwkv_proj_body(*refs, vres):
    (s_ref, sp_ref, sn_ref, mod_ref, gpre_ref, mu_ref, wr_ref, wk_ref, wv_ref,
     w0_ref, w1_ref, w2_ref, a0_ref, a1_ref, a2_ref, g1_ref, g2_ref) = refs[:17]
    refs = refs[17:]
    if vres:
        v0_ref, v1_ref, v2_ref, vf_ref = refs[:4]
        refs = refs[4:]
    r_ref, k_ref, v_ref, g_ref, lw_ref, a_ref = refs

    first, last = _stream_edges()
    h = _pre(s_ref[...], mod_ref, 1, gpre_ref)
    hp = _pre(sp_ref[SUBLANES - 1:SUBLANES, :], mod_ref, 1, gpre_ref)
    hn = _pre(sn_ref[0:1, :], mod_ref, 1, gpre_ref)
    hp = jnp.where(first, 0.0, hp)
    hn = jnp.where(last, 0.0, hn)
    row = lax.broadcasted_iota(jnp.int32, h.shape, 0)
    prev = jnp.where(row == 0, hp, pltpu.roll(h, 1, 0))
    nxt = jnp.where(row == ROW_TILE - 1, hn, pltpu.roll(h, ROW_TILE - 1, 0))
    xx = 0.5 * (prev + nxt) - h

    def mix(n):
        return h + xx * mu_ref[n:n + 1, :]

    r_ref[...] = _bdot(mix(0), wr_ref[...])
    k_ref[...] = _bdot(mix(2), wk_ref[...])
    xvb = mix(3).astype(BF16)
    v = jnp.dot(xvb, wv_ref[...], preferred_element_type=F32)
    if vres:
        gate = _sigmoid(v0_ref[...] + _bdot(jnp.dot(xvb, v1_ref[...], preferred_element_type=F32),
                                            v2_ref[...]))
        v = v + (vf_ref[...] - v) * gate
    v_ref[...] = v
    g_ref[...] = _bdot(_sigmoid(_bdot(mix(5), g1_ref[...])), g2_ref[...])
    lw = w0_ref[...] + _bdot(jnp.tanh(_bdot(mix(1), w1_ref[...])), w2_ref[...])
    logw = -math.exp(-0.5) * _sigmoid(lw)
    aa = _sigmoid(a0_ref[...] + _bdot(_bdot(mix(4), a1_ref[...]), a2_ref[...]))
    for d in range(2):
        lw_ref[d] = logw[:, d * D_MODEL:(d + 1) * D_MODEL]
        a_ref[d] = aa[:, d * D_MODEL:(d + 1) * D_MODEL]


def _lora_pair(w1, w2, w0):
    lora = w1.shape[-1]
    w1c = jnp.concatenate([w1[0], w1[1]], axis=1)
    z = jnp.zeros((lora, D_MODEL), w2.dtype)
    w2c = jnp.concatenate([jnp.concatenate([w2[0], z], axis=1),
                           jnp.concatenate([z, w2[1]], axis=1)], axis=0)
    return w1c.astype(BF16), w2c.astype(BF16), w0.reshape(1, 2 * D_MODEL)


def _pad_to(a, axis, size):
    pad = [(0, 0)] * a.ndim
    pad[axis] = (0, size - a.shape[axis])
    return jnp.pad(a, pad)


def _rwkv_proj(s, mod, gpre, p, v_first):
    n = s.shape[0]
    vres = v_first is not None
    w1c, w2c, w0c = _lora_pair(p['w1'], p['w2'], p['w0'])
    a1c, a2c, a0c = _lora_pair(p['a1'], p['a2'], p['a0'])
    g1 = _pad_to(p['g1'], 1, RW_GATE_LORA_PAD).astype(BF16)
    g2 = _pad_to(p['g2'], 0, RW_GATE_LORA_PAD).astype(BF16)
    prev_spec, next_spec = _halo_specs(n)
    args = [s, s, s, mod, gpre, p['mu'], p['w_r'].astype(BF16), p['w_k'].astype(BF16),
            p['w_v'].astype(BF16), w0c, w1c, w2c, a0c, a1c, a2c, g1, g2]
    specs = [_row_spec(), prev_spec, next_spec, _mod_spec(n)] + [_full(a) for a in args[4:]]
    if vres:
        extra = [p['v0'].reshape(1, D_MODEL), _pad_to(p['v1'], 1, LORA_PAD).astype(BF16),
                 _pad_to(p['v2'], 0, LORA_PAD).astype(BF16)]
        args += extra + [v_first]
        specs += [_full(a) for a in extra] + [_row_spec()]
    row_out = jax.ShapeDtypeStruct((n, D_MODEL), F32)
    dir_out = jax.ShapeDtypeStruct((2, n, D_MODEL), F32)
    dir_spec = pl.BlockSpec((2, ROW_TILE, D_MODEL), lambda i: (0, i, 0))
    return pl.pallas_call(
        functools.partial(_rwkv_proj_body, vres=vres),
        grid=(n // ROW_TILE,),
        in_specs=specs,
        out_specs=[_row_spec()] * 4 + [dir_spec] * 2,
        out_shape=[row_out] * 4 + [dir_out] * 2,
        compiler_params=_cparams(("arbitrary",)),
        name="rwkv_proj",
    )(*args)


WKV_PAIR = LANES // RW_HEAD
WKV_ROWS = WKV_PAIR * WKV_CHUNK
WKV_GROUPS = D_MODEL // LANES
WKV_INV_LEVELS = tuple(range(1, WKV_CHUNK.bit_length() - 1))


def _wkv_masks(sgn):
    c = WKV_CHUNK
    rows = lax.broadcasted_iota(jnp.int32, (WKV_ROWS, LANES), 0)
    lanes = lax.broadcasted_iota(jnp.int32, (WKV_ROWS, LANES), 1)
    keep = (rows < c) == (lanes < RW_HEAD)
    tr = lax.broadcasted_iota(jnp.int32, (WKV_ROWS, WKV_ROWS), 0)
    sr = lax.broadcasted_iota(jnp.int32, (WKV_ROWS, WKV_ROWS), 1)
    same = (tr < c) == (sr < c)
    strict = jnp.logical_and(same, (tr - sr) * sgn > 0)
    incl = jnp.logical_and(same, (tr - sr) * sgn >= 0)
    eye = (tr == sr).astype(F32)
    blk = lambda x, b: lax.shift_right_logical(x, b)
    pairs = [blk(tr, 1) == blk(sr, 1)]
    for b in WKV_INV_LEVELS:
        pairs.append(jnp.logical_and(blk(tr, b + 1) == blk(sr, b + 1), blk(tr, b) != blk(sr, b)))
    return keep, strict, incl, eye, pairs


def _cumsum_rows(x):
    row = lax.broadcasted_iota(jnp.int32, x.shape, 0)
    k = 1
    while k < x.shape[0]:
        x = x + jnp.where(row >= k, pltpu.roll(x, k, 0), 0.0)
        k *= 2
    return x


def _dot_nt(x, y):
    return lax.dot_general(x.astype(BF16), y.astype(BF16), (((1,), (1,)), ((), ())),
                           preferred_element_type=F32)


def _pdot(x, y, prec):
    if prec is None:
        return _bdot(x, y)
    return jnp.dot(x, y, precision=prec, preferred_element_type=F32)


def _wkv_group(lw, cum, a, k, v, r, k_k, k_a, r_k, ln_w, ln_b, st_ref, o_ref, masks,
               prec_inv, prec_state):
    c = WKV_CHUNK
    keep, strict, incl, eye, pairs = masks
    head_a = lax.broadcasted_iota(jnp.int32, (c, LANES), 1) < RW_HEAD

    def head_sum(x):
        sa = jnp.sum(jnp.where(head_a, x, 0.0), axis=-1, keepdims=True)
        sb = jnp.sum(jnp.where(head_a, 0.0, x), axis=-1, keepdims=True)
        return jnp.where(head_a, sa, sb)

    kkf = k * k_k
    kap = kkf * lax.rsqrt(jnp.maximum(head_sum(kkf * kkf), 1e-24))
    kd = k * (1.0 + (a - 1.0) * k_a)
    b = kap * a

    tot = jnp.sum(lw, axis=0, keepdims=True)
    e_neg = jnp.exp(-cum)
    e_rem = jnp.exp(tot - cum)
    kt = kap * jnp.exp(cum - lw)
    rt = r * jnp.exp(cum)
    g_c = jnp.exp(tot)

    def stack(x):
        return jnp.where(keep, jnp.concatenate([x, x], axis=0), 0.0)

    kt_s, rt_s, v_s = stack(kt), stack(rt), stack(v)
    kh_s, bh_s = stack(kd * e_neg), stack(b * e_neg)
    kb_s, bb_s = stack(kd * e_rem), stack(b * e_rem)
    kbt, bbt = kb_s.T.astype(BF16), bb_s.T.astype(BF16)
    bonus = head_sum(r * kd * r_k) * v
    yield

    r2 = WKV_ROWS
    prod = _dot_nt(jnp.concatenate([kt_s, rt_s], axis=0), jnp.concatenate([kh_s, bh_s], axis=0))
    l_k = jnp.where(strict, prod[:r2, :r2], 0.0)
    l_b = jnp.where(strict, prod[:r2, r2:], 0.0)
    p_k = jnp.where(incl, prod[r2:, :r2], 0.0).astype(BF16)
    p_b = jnp.where(incl, prod[r2:, r2:], 0.0).astype(BF16)
    yield

    lkv = _bdot(l_k, v_s)
    t_inv = eye - jnp.where(pairs[0], l_b, 0.0)
    for lvl in range(1, len(pairs)):
        x = _pdot(jnp.where(pairs[lvl], l_b, 0.0), t_inv, prec_inv)
        yield
        t_inv = t_inv - _pdot(t_inv, x, prec_inv)
        yield

    wu = _bdot(t_inv, jnp.concatenate([kt_s, lkv], axis=1))
    yield
    wu = wu.astype(BF16)
    pbwu = jnp.dot(p_b, wu, preferred_element_type=F32)
    q_m = rt_s - pbwu[:, :LANES]
    y_0 = _bdot(p_k, v_s) - pbwu[:, LANES:]
    yield
    bbwu = jnp.dot(bbt, wu, preferred_element_type=F32)
    g_m = eye * g_c - bbwu[:, :LANES]
    h_m = _bdot(kbt, v_s) - bbwu[:, LANES:]
    yield

    a_state = st_ref[...]
    y_s = _pdot(q_m, a_state, prec_state) + y_0
    st_ref[...] = _pdot(g_m, a_state, prec_state) + h_m
    y = y_s[:c] + y_s[c:]

    mu = head_sum(y) * (1.0 / RW_HEAD)
    yc = y - mu
    var = head_sum(yc * yc) * (1.0 / RW_HEAD)
    yn = yc * lax.rsqrt(var + RW_GN_EPS) * ln_w + ln_b
    o_ref[...] = yn + bonus


def _wkv_body(lw_ref, a_ref, k_ref, v_ref, r_ref, kk_ref, ka_ref, rk_ref, lnw_ref, lnb_ref,
              o_ref, st_ref, *, prec_inv, prec_state):
    @pl.when(pl.program_id(1) == 0)
    def _():
        st_ref[...] = jnp.zeros_like(st_ref)

    fwd = pl.program_id(0) == 0
    masks = _wkv_masks(1 - 2 * pl.program_id(0))
    def start_chunk(i):
        groups = []
        first = jnp.where(fwd, i, WKV_STEP_CHUNKS - 1 - i) * WKV_CHUNK
        rows = pl.ds(pl.multiple_of(first, WKV_CHUNK), WKV_CHUNK)
        lw_all = lw_ref[0, rows, :]
        pre = _cumsum_rows(lw_all)
        cum = jnp.where(fwd, pre, pre[WKV_CHUNK - 1:WKV_CHUNK, :] - pre + lw_all)
        for g in range(WKV_GROUPS):
            sl = slice(g * LANES, (g + 1) * LANES)
            groups.append(_wkv_group(lw_ref[0, rows, sl], cum[:, sl], a_ref[0, rows, sl],
                                     k_ref[rows, sl], v_ref[rows, sl], r_ref[rows, sl],
                                     kk_ref[:, sl], ka_ref[:, sl], rk_ref[:, sl], lnw_ref[:, sl],
                                     lnb_ref[:, sl], st_ref.at[g], o_ref.at[0, rows, sl], masks,
                                     prec_inv, prec_state))
        return groups

    chunks = []
    rnd = 0
    while len(chunks) < WKV_STEP_CHUNKS or any(chunks):
        if rnd % WKV_STAGE_LAG == 0 and len(chunks) < WKV_STEP_CHUNKS:
            chunks.append(start_chunk(len(chunks)))
        chunks = [[gen for gen in groups if next(gen, "done") != "done"] for groups in chunks]
        rnd += 1


def _wkv(logw, a, k, v, r, p, prec_inv=None, prec_state=None):
    n = k.shape[0]
    blk = WKV_STEP_CHUNKS * WKV_CHUNK
    assert n % blk == 0 and CTX_LEN % blk == 0
    nb = n // blk
    nbc = CTX_LEN // blk
    nbx = nb - nbc

    def chunk(d, j):
        fwd = jnp.where(j < nbc, nbx + j, j - nbc)
        return jnp.where(d == 0, fwd, nb - 1 - j)

    dir_spec = pl.BlockSpec((1, blk, D_MODEL), lambda d, j: (d, chunk(d, j), 0))
    row_spec = pl.BlockSpec((blk, D_MODEL), lambda d, j: (chunk(d, j), 0))
    vecs = [p['k_k'].reshape(1, D_MODEL), p['k_a'].reshape(1, D_MODEL),
            p['r_k'].reshape(1, D_MODEL), p['ln_w'].reshape(1, D_MODEL),
            p['ln_b'].reshape(1, D_MODEL)]
    return pl.pallas_call(
        functools.partial(_wkv_body, prec_inv=prec_inv, prec_state=prec_state),
        grid=(2, nb),
        in_specs=[dir_spec, dir_spec, row_spec, row_spec, row_spec] + [_full(x) for x in vecs],
        out_specs=dir_spec,
        out_shape=jax.ShapeDtypeStruct((2, n, D_MODEL), F32),
        scratch_shapes=[pltpu.VMEM((WKV_GROUPS, WKV_ROWS, LANES), F32)],
        compiler_params=_cparams(("arbitrary", "arbitrary")),
        name="wkv_scan",
    )(logw, a, k, v, r, *vecs)


def _out_body(*refs, gated):
    if gated:
        o_ref, g_ref, w_ref, s_ref, mod_ref, gpost_ref, out_ref = refs
        y = (o_ref[0] + o_ref[1]) * g_ref[...]
    else:
        o_ref, w_ref, s_ref, mod_ref, gpost_ref, out_ref = refs
        y = o_ref[...]
    y = jnp.dot(y.astype(BF16), w_ref[...], preferred_element_type=F32)
    out_ref[...] = _post(s_ref[...], y, mod_ref, 1, gpost_ref, 1.0)


def _mixer_out(o, g, w_o, s, mod, gpost):
    n = s.shape[0]
    gated = g is not None
    if gated:
        args = [o, g, w_o, s, mod, gpost]
        specs = [pl.BlockSpec((2, ROW_TILE, D_MODEL), lambda i: (0, i, 0)), _row_spec(),
                 _full(w_o), _row_spec(), _mod_spec(n), _full(gpost)]
    else:
        args = [o, w_o, s, mod, gpost]
        specs = [_row_spec(), _full(w_o), _row_spec(), _mod_spec(n), _full(gpost)]
    return pl.pallas_call(
        functools.partial(_out_body, gated=gated),
        grid=(n // ROW_TILE,),
        in_specs=specs,
        out_specs=_row_spec(),
        out_shape=jax.ShapeDtypeStruct(s.shape, F32),
        compiler_params=_cparams(("arbitrary",)),
        name="mixer_out",
    )(*args)


def _mla_proj_body(s_ref, mod_ref, gpre_ref, wdq_ref, qn_ref, wq1_ref, wq2_ref, wdkv_ref, kvn_ref,
                   wkr_ref, wk_ref, wv_ref, place_ref, tq_ref, tk_ref, q_ref, k_ref, v_ref):
    h = _pre(s_ref[...], mod_ref, 1, gpre_ref).astype(BF16)
    cq = jnp.dot(h, wdq_ref[...], preferred_element_type=F32)
    qn = _rms(cq, qn_ref[...]).astype(BF16)
    q1 = jnp.dot(qn, wq1_ref[...], preferred_element_type=F32)
    q2 = jnp.dot(qn, wq2_ref[...], preferred_element_type=F32)
    ckv = jnp.dot(h, wdkv_ref[...], preferred_element_type=F32)
    kvn = _rms(ckv, kvn_ref[...]).astype(BF16)
    kr = jnp.dot(h, wkr_ref[...], preferred_element_type=F32)
    krr = kr * tk_ref[0] + pltpu.roll(kr, LANES - MLA_ROPE, 1) * tk_ref[1]
    kk = (jnp.dot(kvn, wk_ref[...], preferred_element_type=F32)
          + jnp.dot(krr.astype(BF16), place_ref[...], preferred_element_type=F32))
    k_ref[...] = kk.astype(BF16)
    v_ref[0] = jnp.dot(kvn, wv_ref[...], preferred_element_type=F32).T.astype(BF16)
    cos = tq_ref[0]
    sin = tq_ref[1]
    for hd in range(MLA_HEADS):
        sl = slice(hd * MLA_QK_PAD, (hd + 1) * MLA_QK_PAD)
        q_ref[:, sl] = (q1[:, sl] * cos + q2[:, sl] * sin).astype(BF16)


def _rot_cols(w):
    half = MLA_ROPE // 4
    parts = []
    for ax in range(2):
        blk = w[..., ax * 2 * half:(ax + 1) * 2 * half]
        parts += [-blk[..., half:], blk[..., :half]]
    return jnp.concatenate(parts, axis=-1)


def _mla_tables(n):
    t = n - CTX_LEN
    pos = jnp.arange(t, dtype=jnp.int32)
    rowp = (pos // GRID_W).astype(F32)
    colp = (pos % GRID_W).astype(F32)
    axis_dim = MLA_ROPE // 2
    inv = ROPE_BASE ** (-jnp.arange(0, axis_dim, 2, dtype=F32) / axis_dim)
    ar = rowp[:, None] * inv
    ac = colp[:, None] * inv
    cosf = jnp.concatenate([jnp.cos(ar), jnp.cos(ar), jnp.cos(ac), jnp.cos(ac)], axis=-1)
    sinf = jnp.concatenate([jnp.sin(ar), jnp.sin(ar), jnp.sin(ac), jnp.sin(ac)], axis=-1)
    cosf = jnp.concatenate([cosf, jnp.ones((CTX_LEN, MLA_ROPE), F32)], axis=0)
    sinf = jnp.concatenate([sinf, jnp.zeros((CTX_LEN, MLA_ROPE), F32)], axis=0)
    scale = (MLA_NOPE + MLA_ROPE) ** -0.5 * math.log2(math.e)
    zq = jnp.zeros((n, MLA_QK_PAD - MLA_NOPE - MLA_ROPE), F32)
    tq = jnp.stack([jnp.concatenate([jnp.ones((n, MLA_NOPE), F32), cosf, zq], axis=-1),
                    jnp.concatenate([jnp.zeros((n, MLA_NOPE), F32), sinf, zq], axis=-1)]) * scale
    zk = jnp.zeros((n, LANES - MLA_ROPE), F32)
    tk = jnp.stack([jnp.concatenate([cosf, zk], axis=-1), jnp.concatenate([sinf, zk], axis=-1)])
    return tq, tk


def _mla_proj(s, mod, gpre, w_dq, q_norm, w_uq, w_dkv, kv_norm, w_ukv, key_tile):
    n = s.shape[0]
    per = key_tile // ROW_TILE
    hq = MLA_NOPE + MLA_ROPE
    wq = w_uq.reshape(MLA_Q_RANK, MLA_HEADS, hq)
    zq = jnp.zeros((MLA_Q_RANK, MLA_HEADS, MLA_QK_PAD - hq), w_uq.dtype)
    wq1 = jnp.concatenate([wq, zq], axis=-1).reshape(MLA_Q_RANK, MLA_HEADS * MLA_QK_PAD)
    wq2 = jnp.concatenate([jnp.zeros_like(wq[..., :MLA_NOPE]), _rot_cols(wq[..., MLA_NOPE:]), zq],
                          axis=-1).reshape(MLA_Q_RANK, MLA_HEADS * MLA_QK_PAD)
    wkv = w_ukv.reshape(MLA_KV_RANK, MLA_HEADS, MLA_NOPE + MLA_V)
    wk = jnp.concatenate([wkv[..., :MLA_NOPE],
                          jnp.zeros((MLA_KV_RANK, MLA_HEADS, MLA_QK_PAD - MLA_NOPE), w_ukv.dtype)],
                         axis=-1).reshape(MLA_KV_RANK, MLA_HEADS * MLA_QK_PAD)
    wv = wkv[..., MLA_NOPE:].reshape(MLA_KV_RANK, MLA_HEADS * MLA_V)
    w_rope = w_dkv[:, MLA_KV_RANK:]
    wkr = jnp.concatenate([w_rope, _rot_cols(w_rope),
                           jnp.zeros((D_MODEL, LANES - 2 * MLA_ROPE), w_dkv.dtype)], axis=-1)
    src = jnp.arange(LANES)[:, None]
    dst = jnp.arange(MLA_HEADS * MLA_QK_PAD)[None, :]
    place = jnp.logical_and(src < MLA_ROPE, dst % MLA_QK_PAD == MLA_NOPE + src).astype(BF16)
    tq, tk = _mla_tables(n)
    weights = [w_dq.astype(BF16), q_norm.reshape(1, MLA_Q_RANK), wq1.astype(BF16), wq2.astype(BF16),
               w_dkv[:, :MLA_KV_RANK].astype(BF16), kv_norm.reshape(1, MLA_KV_RANK),
               wkr.astype(BF16), wk.astype(BF16), wv.astype(BF16), place]
    tab_spec = pl.BlockSpec((2, ROW_TILE, LANES), lambda i: (0, i, 0))
    return pl.pallas_call(
        _mla_proj_body,
        grid=(n // ROW_TILE,),
        in_specs=[_row_spec(), _mod_spec(n), _full(gpre)] + [_full(w) for w in weights]
                 + [tab_spec, tab_spec],
        out_specs=[_row_spec(MLA_HEADS * MLA_QK_PAD), _row_spec(MLA_HEADS * MLA_QK_PAD),
                   pl.BlockSpec((1, MLA_HEADS * MLA_V, ROW_TILE), lambda i: (i // per, 0, i % per))],
        out_shape=[jax.ShapeDtypeStruct((n, MLA_HEADS * MLA_QK_PAD), BF16),
                   jax.ShapeDtypeStruct((n, MLA_HEADS * MLA_QK_PAD), BF16),
                   jax.ShapeDtypeStruct((n // key_tile, MLA_HEADS * MLA_V, key_tile), BF16)],
        compiler_params=_cparams(("arbitrary",)),
        name="mla_proj",
    )(s, mod, gpre, *weights, tq, tk)


MLA_PAIR = LANES // MLA_V


ATTN_KEY_BLOCK = 16
ATTN_SAFE_LOG2 = 90.0


def _attn_body(q_ref, k_ref, vt_ref, o_ref, m_ref, l_ref, acc_ref, s_ref, p_ref, kb_ref, *, tk, nk):
    tq = q_ref.shape[0]
    l_ref[...] = jnp.zeros_like(l_ref)
    acc_ref[...] = jnp.zeros_like(acc_ref)

    def row_norm2_max(x):
        xf = x.astype(F32)
        return jnp.max(jnp.sum(xf * xf, axis=-1, keepdims=True), axis=0, keepdims=True)

    @pl.when(pl.program_id(1) == 0)
    def _():
        for hd in range(MLA_PAIR):
            sl = slice(hd * MLA_QK_PAD, (hd + 1) * MLA_QK_PAD)

            def key_tile(t, best):
                rows = pl.ds(pl.multiple_of(t * tk, tk), tk)
                return jnp.maximum(best, row_norm2_max(k_ref[rows, sl]))

            kb_ref[hd] = lax.fori_loop(0, nk, key_tile, jnp.zeros((1, 1), F32))

    bound2 = jnp.zeros((1, 1), F32)
    for hd in range(MLA_PAIR):
        sl = slice(hd * MLA_QK_PAD, (hd + 1) * MLA_QK_PAD)
        bound2 = jnp.maximum(bound2, row_norm2_max(q_ref[:, sl]) * kb_ref[hd])
    small = bound2[0, 0] <= ATTN_SAFE_LOG2 ** 2

    @pl.when(small)
    def _():
        def tile(t, carry):
            rows = pl.ds(pl.multiple_of(t * tk, tk), tk)
            for hd in range(MLA_PAIR):
                sl = slice(hd * MLA_QK_PAD, (hd + 1) * MLA_QK_PAD)
                s = lax.dot_general(k_ref[rows, sl], q_ref[:, sl], (((1,), (1,)), ((), ())),
                                    preferred_element_type=F32)
                p = jnp.exp2(s)
                l_ref[hd] += jnp.sum(p.reshape(tk // SUBLANES, SUBLANES, tq), axis=0)
                vt = vt_ref[t, hd * MLA_V:(hd + 1) * MLA_V, :]
                acc_ref[hd] += jnp.dot(vt, p.astype(BF16), preferred_element_type=F32)
            return carry

        lax.fori_loop(0, nk, tile, 0)

    @pl.when(jnp.logical_not(small))
    def _():
        _attn_running_max(q_ref, k_ref, vt_ref, m_ref, l_ref, acc_ref, s_ref, p_ref, tk=tk, nk=nk)

    outs = [acc_ref[hd] / jnp.sum(l_ref[hd], axis=0, keepdims=True) for hd in range(MLA_PAIR)]
    o_ref[...] = jnp.concatenate(outs, axis=0).T.astype(o_ref.dtype)


def _attn_running_max(q_ref, k_ref, vt_ref, m_ref, l_ref, acc_ref, s_ref, p_ref, *, tk, nk):
    tq = q_ref.shape[0]
    m_ref[...] = jnp.full_like(m_ref, -jnp.inf)

    def tile(t, carry):
        rows = pl.ds(pl.multiple_of(t * tk, tk), tk)
        for hd in range(MLA_PAIR):
            sl = slice(hd * MLA_QK_PAD, (hd + 1) * MLA_QK_PAD)
            s_ref[hd] = lax.dot_general(k_ref[rows, sl], q_ref[:, sl], (((1,), (1,)), ((), ())),
                                        preferred_element_type=F32)
        for hd in range(MLA_PAIR):
            mx = s_ref[hd, 0:SUBLANES, :]
            for j in range(1, tk // SUBLANES):
                mx = jnp.maximum(mx, s_ref[hd, j * SUBLANES:(j + 1) * SUBLANES, :])
            m_old = m_ref[hd]
            m_new = jnp.maximum(m_old, jnp.max(mx, axis=0, keepdims=True))
            m_ref[hd] = m_new
            alpha = jnp.exp2(m_old - m_new)
            m_rows = jnp.broadcast_to(m_new, (ATTN_KEY_BLOCK, tq))
            lsum = alpha * l_ref[hd]
            for b in range(tk // ATTN_KEY_BLOCK):
                r = slice(b * ATTN_KEY_BLOCK, (b + 1) * ATTN_KEY_BLOCK)
                p = jnp.exp2(s_ref[hd, r, :] - m_rows)
                lsum = lsum + (p[:SUBLANES] + p[SUBLANES:])
                p_ref[hd, r, :] = p.astype(BF16)
            l_ref[hd] = lsum
            vt = vt_ref[t, hd * MLA_V:(hd + 1) * MLA_V, :]
            acc_ref[hd] = alpha * acc_ref[hd] + jnp.dot(vt, p_ref[hd], preferred_element_type=F32)
        return carry

    lax.fori_loop(0, nk, tile, 0)


def _attention(q, k, vt, q_row0, q_rows, k_row0, k_rows, tq, tk):
    vtile = vt.shape[2]
    assert q_row0 % tq == 0 and q_rows % tq == 0 and k_row0 % k_rows == 0 and k_rows % tk == 0
    q0 = q_row0 // tq
    k0 = k_row0 // k_rows
    pair_w = MLA_PAIR * MLA_QK_PAD
    if k_rows % vtile == 0:
        assert tk == vtile
        vt_spec = pl.BlockSpec((k_rows // vtile, LANES, vtile), lambda p, i: (k_row0 // k_rows, p, 0))
    else:
        assert vtile % k_rows == 0 and tk == k_rows
        vt_spec = pl.BlockSpec((1, LANES, k_rows),
                               lambda p, i: (k_row0 // vtile, p, (k_row0 % vtile) // k_rows))
    return pl.pallas_call(
        functools.partial(_attn_body, tk=tk, nk=k_rows // tk),
        grid=(MLA_HEADS // MLA_PAIR, q_rows // tq),
        in_specs=[pl.BlockSpec((tq, pair_w), lambda p, i: (i + q0, p)),
                  pl.BlockSpec((k_rows, pair_w), lambda p, i: (k0, p)),
                  vt_spec],
        out_specs=pl.BlockSpec((tq, LANES), lambda p, i: (i, p)),
        out_shape=jax.ShapeDtypeStruct((q_rows, MLA_HEADS * MLA_V), BF16),
        scratch_shapes=[pltpu.VMEM((MLA_PAIR, 1, tq), F32), pltpu.VMEM((MLA_PAIR, SUBLANES, tq), F32),
                        pltpu.VMEM((MLA_PAIR, MLA_V, tq), F32), pltpu.VMEM((MLA_PAIR, tk, tq), F32),
                        pltpu.VMEM((MLA_PAIR, tk, tq), BF16), pltpu.VMEM((MLA_PAIR, 1, 1), F32)],
        compiler_params=_cparams(("arbitrary", "arbitrary")),
        name="mla_attention",
    )(q, k, vt)


def _largest_divisor(n, candidates):
    for c in candidates:
        if n % c == 0:
            return c
    raise ValueError(f"no tile in {candidates} divides {n}")


def _pool_body(s_ref, sp_ref, sn_ref, mod_ref, gpre_ref, gpost_ref, w_ref, b_ref, sc_ref, o_ref):
    i = pl.program_id(0)
    nt = pl.num_programs(0)
    nct = CTX_LEN // ROW_TILE
    first, last = _stream_edges()
    s = s_ref[...]
    h = _pre(s, mod_ref, 1, gpre_ref)
    hp = jnp.where(first, 0.0, _pre(sp_ref[...], mod_ref, 1, gpre_ref))
    hn = jnp.where(last, 0.0, _pre(sn_ref[...], mod_ref, 1, gpre_ref))
    ext = jnp.concatenate([hp, h, hn], axis=0)
    er = ROW_TILE + 2 * POOL_HALO
    in_ctx = i >= nt - nct
    t0 = jnp.where(in_ctx, i - (nt - nct), i) * ROW_TILE
    t_len = jnp.where(in_ctx, nct, nt - nct) * ROW_TILE
    t = t0 + lax.broadcasted_iota(jnp.int32, (ROW_TILE, 1), 0)

    outs = []
    run = ext
    width = 1
    for gi, win in enumerate(POOL_WINDOWS):
        while width < win:
            run = run + pltpu.roll(run, width, 0)
            width *= 2
        ahead = win // 2 - 1
        grp = run[:, :POOL_GROUP]
        if ahead:
            grp = pltpu.roll(grp, er - ahead, 0)
        wsum = grp[POOL_HALO:POOL_HALO + ROW_TILE]
        cnt = (jnp.minimum(t + win // 2, t_len) - jnp.maximum(t - win // 2, 0)).astype(F32)
        diff = wsum / cnt - h[:, gi * POOL_GROUP:(gi + 1) * POOL_GROUP]
        outs.append(_bdot(diff, w_ref[gi]) + b_ref[gi:gi + 1, :])
        run = run[:, POOL_GROUP:]
    y = jnp.concatenate(outs, axis=-1) * sc_ref[...]
    o_ref[...] = _post(s, y, mod_ref, 1, gpost_ref, 1.0)


def _pool(s, mod, gpre, gpost, w, b, scale):
    n = s.shape[0]
    prev_spec, next_spec = _halo_specs(n)
    sc = scale.reshape(1, D_MODEL)
    wb = w.astype(BF16)
    return pl.pallas_call(
        _pool_body,
        grid=(n // ROW_TILE,),
        in_specs=[_row_spec(), prev_spec, next_spec, _mod_spec(n), _full(gpre), _full(gpost),
                  _full(wb), _full(b), _full(sc)],
        out_specs=_row_spec(),
        out_shape=jax.ShapeDtypeStruct(s.shape, F32),
        compiler_params=_cparams(("arbitrary",)),
        name="pool_mixer",
    )(s, s, s, mod, gpre, gpost, wb, b, sc)


def kernel(x, c, ctx, c_ctx, mod_w, mod_b, norm_pre, norm_post, ffn_w_gate, ffn_w_up, ffn_w_down,
           rw_mu, rw_w_r, rw_w_k, rw_w_v, rw_w_o, rw_w0, rw_w1, rw_w2, rw_a0, rw_a1, rw_a2,
           rw_v0, rw_v1, rw_v2, rw_g1, rw_g2, rw_k_k, rw_k_a, rw_r_k, rw_ln_w, rw_ln_b,
           mla_w_dq, mla_q_norm, mla_w_uq, mla_w_dkv, mla_kv_norm, mla_w_ukv, mla_w_o,
           pool_w, pool_b, pool_scale):
    batch, t, d = x.shape
    assert batch == 1 and d == D_MODEL and ctx.shape == (1, CTX_LEN, D_MODEL)
    assert t % ROW_TILE == 0 and CTX_LEN % ROW_TILE == 0 and t % GRID_W == 0
    n = t + CTX_LEN
    s = jnp.concatenate([x[0], ctx[0]], axis=0)
    c2 = jnp.concatenate([c_ctx[None], c, jnp.zeros((SUBLANES - 2, D_MODEL), F32)], axis=0)
    mod_all = _modulation(c2, mod_w, mod_b)
    wg_all, wu_all, wd_all = (w.astype(BF16) for w in (ffn_w_gate, ffn_w_up, ffn_w_down))
    v_first = None
    for i in range(DEPTH):
        kind, j = i % 3, i // 3
        mod = mod_all[i]
        gpre, gpost = norm_pre[i], norm_post[i]
        s = _ffn(s, mod, gpre, gpost, wg_all, wu_all, wd_all, i, 0, 0, n)
        if kind == 0:
            p = {'mu': rw_mu[j], 'w_r': rw_w_r[j], 'w_k': rw_w_k[j], 'w_v': rw_w_v[j],
                 'w0': rw_w0[j], 'w1': rw_w1[j], 'w2': rw_w2[j],
                 'a0': rw_a0[j], 'a1': rw_a1[j], 'a2': rw_a2[j],
                 'g1': rw_g1[j], 'g2': rw_g2[j], 'k_k': rw_k_k[j], 'k_a': rw_k_a[j],
                 'r_k': rw_r_k[j], 'ln_w': rw_ln_w[j], 'ln_b': rw_ln_b[j]}
            if j > 0:
                p['v0'], p['v1'], p['v2'] = rw_v0[j - 1], rw_v1[j - 1], rw_v2[j - 1]
            r, k, v, g, logw, a = _rwkv_proj(s, mod, gpre, p, v_first if j > 0 else None)
            if j == 0:
                v_first = v
            o = _wkv(logw, a, k, v, r, p)
            s = _mixer_out(o, g, rw_w_o[j].astype(BF16), s, mod, gpost)
        elif kind == 1:
            tq = _largest_divisor(t, (1024, 512, 256))
            tk = _largest_divisor(n, (1280, 256))
            q, k, vt = _mla_proj(s, mod, gpre, mla_w_dq[j], mla_q_norm[j], mla_w_uq[j],
                                 mla_w_dkv[j], mla_kv_norm[j], mla_w_ukv[j], tk)
            o_x = _attention(q, k, vt, 0, t, 0, n, tq, tk)
            o_c = _attention(q, k, vt, t, CTX_LEN, t, CTX_LEN, CTX_LEN, CTX_LEN)
            o = jnp.concatenate([o_x, o_c], axis=0)
            s = _mixer_out(o, None, mla_w_o[j].astype(BF16), s, mod, gpost)
        else:
            s = _pool(s, mod, gpre, gpost, pool_w[j], pool_b[j], pool_scale[j])
        s = _ffn(s, mod, gpre, gpost, wg_all, wu_all, wd_all, i, 1, 2, t if i == DEPTH - 1 else n)
    return s[None]
```

```python
import functools
import math

import jax
import jax.numpy as jnp
from jax import lax
from jax.experimental import pallas as pl
from jax.experimental.pallas import tpu as pltpu

F32 = jnp.float32
BF16 = jnp.bfloat16

D_MODEL = 1024
DEPTH = 4
CTX_LEN = 256
GRID_W = 64
N_MOD = 9
FFN_RES = 0.5
NORM_EPS = 1e-6

RW_HEAD = 64
RW_GN_EPS = 64e-5
RW_GATE_LORA_PAD = 256
LORA_PAD = 128

MLA_HEADS = 16
MLA_NOPE = 64
MLA_ROPE = 32
MLA_V = 64
MLA_Q_RANK = 384
MLA_KV_RANK = 256
ROPE_BASE = 10000.0
MLA_QK_PAD = 128

POOL_WINDOWS = (2, 4, 8, 16)
POOL_GROUP = D_MODEL // len(POOL_WINDOWS)
POOL_HALO = 8

LANES = 128
SUBLANES = 8
ROW_TILE = 256
WKV_CHUNK = 64
WKV_STEP_CHUNKS = 4
WKV_STAGE_LAG = 2
VMEM_LIMIT = 56 * 1024 * 1024


def _cparams(sem):
    return pltpu.CompilerParams(dimension_semantics=sem, vmem_limit_bytes=VMEM_LIMIT)


def _full(a):
    nd = a.ndim
    return pl.BlockSpec(a.shape, lambda *_: (0,) * nd)


def _sigmoid(x):
    return 1.0 / (1.0 + jnp.exp(-x))


def _rms(x, g):
    return x * lax.rsqrt(jnp.mean(x * x, axis=-1, keepdims=True) + NORM_EPS) * g


def _pre(s, mod_ref, slot, gpre_ref):
    shift = mod_ref[0, 3 * slot:3 * slot + 1, :]
    scale = mod_ref[0, 3 * slot + 1:3 * slot + 2, :]
    return _rms(s, gpre_ref[slot:slot + 1, :]) * (1.0 + scale) + shift


def _post(s, y, mod_ref, slot, gpost_ref, weight):
    gate = mod_ref[0, 3 * slot + 2:3 * slot + 3, :]
    return s + (weight * gate) * _rms(y, gpost_ref[slot:slot + 1, :])


def _bdot(a, b):
    return jnp.dot(a.astype(BF16), b.astype(BF16), preferred_element_type=F32)


def _row_spec(width=D_MODEL):
    return pl.BlockSpec((ROW_TILE, width), lambda i: (i, 0))


def _mod_spec(n_rows):
    nxt = (n_rows - CTX_LEN) // ROW_TILE
    return pl.BlockSpec((1, N_MOD, D_MODEL), lambda i: (jnp.where(i >= nxt, 0, 1), 0, 0))


def _halo_specs(n_rows):
    per = ROW_TILE // SUBLANES
    nblk = n_rows // SUBLANES
    prev = pl.BlockSpec((SUBLANES, D_MODEL), lambda i: (jnp.maximum(i * per - 1, 0), 0))
    nxt = pl.BlockSpec((SUBLANES, D_MODEL), lambda i: (jnp.minimum((i + 1) * per, nblk - 1), 0))
    return prev, nxt


def _stream_edges():
    i = pl.program_id(0)
    nt = pl.num_programs(0)
    nxt = nt - CTX_LEN // ROW_TILE
    first = jnp.logical_or(i == 0, i == nxt)
    last = jnp.logical_or(i == nxt - 1, i == nt - 1)
    return first, last


def _mod_body(c_ref, w_ref, b_ref, o_ref):
    c = c_ref[...]
    s = c * _sigmoid(c)
    o_ref[0] = _bdot(s, w_ref[0]) + b_ref[0]


def _modulation(c2, mod_w, mod_b):
    tn = D_MODEL
    out = pl.pallas_call(
        _mod_body,
        grid=(DEPTH, N_MOD * D_MODEL // tn),
        in_specs=[pl.BlockSpec((SUBLANES, D_MODEL), lambda l, j: (0, 0)),
                  pl.BlockSpec((1, D_MODEL, tn), lambda l, j: (l, 0, j)),
                  pl.BlockSpec((1, 1, tn), lambda l, j: (l, 0, j))],
        out_specs=pl.BlockSpec((1, SUBLANES, tn), lambda l, j: (l, 0, j)),
        out_shape=jax.ShapeDtypeStruct((DEPTH, SUBLANES, N_MOD * D_MODEL), F32),
        compiler_params=_cparams(("arbitrary", "arbitrary")),
        name="modulation",
    )(c2, mod_w, mod_b.reshape(DEPTH, 1, N_MOD * D_MODEL))
    return out[:, :2].reshape(DEPTH, 2, N_MOD, D_MODEL)


def _ffn_body(s_ref, mod_ref, gpre_ref, gpost_ref, wg_ref, wu_ref, wd_ref, o_ref, *, slot):
    s = s_ref[...]
    h = _pre(s, mod_ref, slot, gpre_ref).astype(BF16)
    g = jnp.dot(h, wg_ref[...], preferred_element_type=F32)
    u = jnp.dot(h, wu_ref[...], preferred_element_type=F32)
    act = (g * _sigmoid(g)) * u
    y = jnp.dot(act.astype(BF16), wd_ref[...], preferred_element_type=F32)
    o_ref[...] = _post(s, y, mod_ref, slot, gpost_ref, FFN_RES)


def _ffn(s, mod, gpre, gpost, wg, wu, wd, layer, which, slot, out_rows):
    n = s.shape[0]

    def weight_spec(w):
        return pl.BlockSpec((None, None) + w.shape[2:], lambda i: (layer, which, 0, 0))

    return pl.pallas_call(
        functools.partial(_ffn_body, slot=slot),
        grid=(out_rows // ROW_TILE,),
        in_specs=[_row_spec(), _mod_spec(n), _full(gpre), _full(gpost),
                  weight_spec(wg), weight_spec(wu), weight_spec(wd)],
        out_specs=_row_spec(),
        out_shape=jax.ShapeDtypeStruct((out_rows, D_MODEL), F32),
        compiler_params=_cparams(("arbitrary",)),
        name="ffn",
    )(s, mod, gpre, gpost, wg, wu, wd)


def _rwkv_proj_body(*refs, vres):
    (s_ref, sp_ref, sn_ref, mod_ref, gpre_ref, mu_ref, wr_ref, wk_ref, wv_ref,
     w0_ref, w1_ref, w2_ref, a0_ref, a1_ref, a2_ref, g1_ref, g2_ref) = refs[:17]
    refs = refs[17:]
    if vres:
        v0_ref, v1_ref, v2_ref, vf_ref = refs[:4]
        refs = refs[4:]
    r_ref, k_ref, v_ref, g_ref, lw_ref, a_ref = refs

    first, last = _stream_edges()
    h = _pre(s_ref[...], mod_ref, 1, gpre_ref)
    hp = _pre(sp_ref[SUBLANES - 1:SUBLANES, :], mod_ref, 1, gpre_ref)
    hn = _pre(sn_ref[0:1, :], mod_ref, 1, gpre_ref)
    hp = jnp.where(first, 0.0, hp)
    hn = jnp.where(last, 0.0, hn)
    row = lax.broadcasted_iota(jnp.int32, h.shape, 0)
    prev = jnp.where(row == 0, hp, pltpu.roll(h, 1, 0))
    nxt = jnp.where(row == ROW_TILE - 1, hn, pltpu.roll(h, ROW_TILE - 1, 0))
    xx = 0.5 * (prev + nxt) - h

    def mix(n):
        return h + xx * mu_ref[n:n + 1, :]

    r_ref[...] = _bdot(mix(0), wr_ref[...])
    k_ref[...] = _bdot(mix(2), wk_ref[...])
    xvb = mix(3).astype(BF16)
    v = jnp.dot(xvb, wv_ref[...], preferred_element_type=F32)
    if vres:
        gate = _sigmoid(v0_ref[...] + _bdot(jnp.dot(xvb, v1_ref[...], preferred_element_type=F32),
                                            v2_ref[...]))
        v = v + (vf_ref[...] - v) * gate
    v_ref[...] = v
    g_ref[...] = _bdot(_sigmoid(_bdot(mix(5), g1_ref[...])), g2_ref[...]).astype(BF16)
    lw = w0_ref[...] + _bdot(jnp.tanh(_bdot(mix(1), w1_ref[...])), w2_ref[...])
    logw = -math.exp(-0.5) * _sigmoid(lw)
    aa = _sigmoid(a0_ref[...] + _bdot(_bdot(mix(4), a1_ref[...]), a2_ref[...]))
    for d in range(2):
        lw_ref[d] = logw[:, d * D_MODEL:(d + 1) * D_MODEL]
        a_ref[d] = aa[:, d * D_MODEL:(d + 1) * D_MODEL]


def _lora_pair(w1, w2, w0):
    lora = w1.shape[-1]
    w1c = jnp.concatenate([w1[0], w1[1]], axis=1)
    z = jnp.zeros((lora, D_MODEL), w2.dtype)
    w2c = jnp.concatenate([jnp.concatenate([w2[0], z], axis=1),
                           jnp.concatenate([z, w2[1]], axis=1)], axis=0)
    return w1c.astype(BF16), w2c.astype(BF16), w0.reshape(1, 2 * D_MODEL)


def _pad_to(a, axis, size):
    pad = [(0, 0)] * a.ndim
    pad[axis] = (0, size - a.shape[axis])
    return jnp.pad(a, pad)


def _rwkv_proj(s, mod, gpre, p, v_first):
    n = s.shape[0]
    vres = v_first is not None
    w1c, w2c, w0c = _lora_pair(p['w1'], p['w2'], p['w0'])
    a1c, a2c, a0c = _lora_pair(p['a1'], p['a2'], p['a0'])
    g1 = _pad_to(p['g1'], 1, RW_GATE_LORA_PAD).astype(BF16)
    g2 = _pad_to(p['g2'], 0, RW_GATE_LORA_PAD).astype(BF16)
    prev_spec, next_spec = _halo_specs(n)
    args = [s, s, s, mod, gpre, p['mu'], p['w_r'].astype(BF16), p['w_k'].astype(BF16),
            p['w_v'].astype(BF16), w0c, w1c, w2c, a0c, a1c, a2c, g1, g2]
    specs = [_row_spec(), prev_spec, next_spec, _mod_spec(n)] + [_full(a) for a in args[4:]]
    if vres:
        extra = [p['v0'].reshape(1, D_MODEL), _pad_to(p['v1'], 1, LORA_PAD).astype(BF16),
                 _pad_to(p['v2'], 0, LORA_PAD).astype(BF16)]
        args += extra + [v_first]
        specs += [_full(a) for a in extra] + [_row_spec()]
    row_out = jax.ShapeDtypeStruct((n, D_MODEL), F32)
    dir_out = jax.ShapeDtypeStruct((2, n, D_MODEL), F32)
    dir_spec = pl.BlockSpec((2, ROW_TILE, D_MODEL), lambda i: (0, i, 0))
    return pl.pallas_call(
        functools.partial(_rwkv_proj_body, vres=vres),
        grid=(n // ROW_TILE,),
        in_specs=specs,
        out_specs=[_row_spec()] * 4 + [dir_spec] * 2,
        out_shape=[row_out] * 3 + [jax.ShapeDtypeStruct((n, D_MODEL), BF16)] + [dir_out] * 2,
        compiler_params=_cparams(("arbitrary",)),
        name="rwkv_proj",
    )(*args)


WKV_PAIR = LANES // RW_HEAD
WKV_ROWS = WKV_PAIR * WKV_CHUNK
WKV_GROUPS = D_MODEL // LANES
WKV_INV_LEVELS = tuple(range(1, WKV_CHUNK.bit_length() - 1))


def _wkv_masks(sgn):
    c = WKV_CHUNK
    rows = lax.broadcasted_iota(jnp.int32, (WKV_ROWS, LANES), 0)
    lanes = lax.broadcasted_iota(jnp.int32, (WKV_ROWS, LANES), 1)
    keep = (rows < c) == (lanes < RW_HEAD)
    tr = lax.broadcasted_iota(jnp.int32, (WKV_ROWS, WKV_ROWS), 0)
    sr = lax.broadcasted_iota(jnp.int32, (WKV_ROWS, WKV_ROWS), 1)
    same = (tr < c) == (sr < c)
    strict = jnp.logical_and(same, (tr - sr) * sgn > 0)
    incl = jnp.logical_and(same, (tr - sr) * sgn >= 0)
    eye = (tr == sr).astype(F32)
    blk = lambda x, b: lax.shift_right_logical(x, b)
    pairs = [blk(tr, 1) == blk(sr, 1)]
    for b in WKV_INV_LEVELS:
        pairs.append(jnp.logical_and(blk(tr, b + 1) == blk(sr, b + 1), blk(tr, b) != blk(sr, b)))
    return keep, strict, incl, eye, pairs


def _cumsum_rows(x):
    row = lax.broadcasted_iota(jnp.int32, x.shape, 0)
    k = 1
    while k < x.shape[0]:
        x = x + jnp.where(row >= k, pltpu.roll(x, k, 0), 0.0)
        k *= 2
    return x


def _dot_nt(x, y):
    return lax.dot_general(x.astype(BF16), y.astype(BF16), (((1,), (1,)), ((), ())),
                           preferred_element_type=F32)


def _pdot(x, y, prec):
    if prec is None:
        return _bdot(x, y)
    return jnp.dot(x, y, precision=prec, preferred_element_type=F32)


def _wkv_group(lw, cum, a, k, v, r, k_k, k_a, r_k, ln_w, ln_b, st_ref, o_ref, masks,
               prec_inv, prec_state):
    c = WKV_CHUNK
    keep, strict, incl, eye, pairs = masks
    head_a = lax.broadcasted_iota(jnp.int32, (c, LANES), 1) < RW_HEAD

    def head_sum(x):
        sa = jnp.sum(jnp.where(head_a, x, 0.0), axis=-1, keepdims=True)
        sb = jnp.sum(jnp.where(head_a, 0.0, x), axis=-1, keepdims=True)
        return jnp.where(head_a, sa, sb)

    kkf = k * k_k
    kap = kkf * lax.rsqrt(jnp.maximum(head_sum(kkf * kkf), 1e-24))
    kd = k * (1.0 + (a - 1.0) * k_a)
    b = kap * a

    tot = jnp.sum(lw, axis=0, keepdims=True)
    e_neg = jnp.exp(-cum)
    e_rem = jnp.exp(tot - cum)
    kt = kap * jnp.exp(cum - lw)
    rt = r * jnp.exp(cum)
    g_c = jnp.exp(tot)

    def stack(x):
        return jnp.where(keep, jnp.concatenate([x, x], axis=0), 0.0)

    kt_s, rt_s, v_s = stack(kt), stack(rt), stack(v)
    kh_s, bh_s = stack(kd * e_neg), stack(b * e_neg)
    kb_s, bb_s = stack(kd * e_rem), stack(b * e_rem)
    kbt, bbt = kb_s.T.astype(BF16), bb_s.T.astype(BF16)
    bonus = head_sum(r * kd * r_k) * v
    yield

    r2 = WKV_ROWS
    prod = _dot_nt(jnp.concatenate([kt_s, rt_s], axis=0), jnp.concatenate([kh_s, bh_s], axis=0))
    l_k = jnp.where(strict, prod[:r2, :r2], 0.0)
    l_b = jnp.where(strict, prod[:r2, r2:], 0.0)
    p_k = jnp.where(incl, prod[r2:, :r2], 0.0).astype(BF16)
    p_b = jnp.where(incl, prod[r2:, r2:], 0.0).astype(BF16)
    yield

    lkv = _bdot(l_k, v_s)
    t_inv = eye - jnp.where(pairs[0], l_b, 0.0)
    for lvl in range(1, len(pairs)):
        x = _pdot(jnp.where(pairs[lvl], l_b, 0.0), t_inv, prec_inv)
        yield
        t_inv = t_inv - _pdot(t_inv, x, prec_inv)
        yield

    wu = _bdot(t_inv, jnp.concatenate([kt_s, lkv], axis=1))
    yield
    wu = wu.astype(BF16)
    pbwu = jnp.dot(p_b, wu, preferred_element_type=F32)
    q_m = rt_s - pbwu[:, :LANES]
    y_0 = _bdot(p_k, v_s) - pbwu[:, LANES:]
    yield
    bbwu = jnp.dot(bbt, wu, preferred_element_type=F32)
    g_m = eye * g_c - bbwu[:, :LANES]
    h_m = _bdot(kbt, v_s) - bbwu[:, LANES:]
    yield

    a_state = st_ref[...]
    y_s = _pdot(q_m, a_state, prec_state) + y_0
    st_ref[...] = _pdot(g_m, a_state, prec_state) + h_m
    y = y_s[:c] + y_s[c:]

    mu = head_sum(y) * (1.0 / RW_HEAD)
    yc = y - mu
    var = head_sum(yc * yc) * (1.0 / RW_HEAD)
    yn = yc * lax.rsqrt(var + RW_GN_EPS) * ln_w + ln_b
    o_ref[...] = (yn + bonus).astype(o_ref.dtype)


def _wkv_body(lw_ref, a_ref, k_ref, v_ref, r_ref, kk_ref, ka_ref, rk_ref, lnw_ref, lnb_ref,
              o_ref, st_ref, *, prec_inv, prec_state):
    @pl.when(pl.program_id(1) == 0)
    def _():
        st_ref[...] = jnp.zeros_like(st_ref)

    fwd = pl.program_id(0) == 0
    masks = _wkv_masks(1 - 2 * pl.program_id(0))
    def start_chunk(i):
        groups = []
        first = jnp.where(fwd, i, WKV_STEP_CHUNKS - 1 - i) * WKV_CHUNK
        rows = pl.ds(pl.multiple_of(first, WKV_CHUNK), WKV_CHUNK)
        lw_all = lw_ref[0, rows, :]
        pre = _cumsum_rows(lw_all)
        cum = jnp.where(fwd, pre, pre[WKV_CHUNK - 1:WKV_CHUNK, :] - pre + lw_all)
        for g in range(WKV_GROUPS):
            sl = slice(g * LANES, (g + 1) * LANES)
            groups.append(_wkv_group(lw_ref[0, rows, sl], cum[:, sl], a_ref[0, rows, sl],
                                     k_ref[rows, sl], v_ref[rows, sl], r_ref[rows, sl],
                                     kk_ref[:, sl], ka_ref[:, sl], rk_ref[:, sl], lnw_ref[:, sl],
                                     lnb_ref[:, sl], st_ref.at[g], o_ref.at[0, rows, sl], masks,
                                     prec_inv, prec_state))
        return groups

    chunks = []
    rnd = 0
    while len(chunks) < WKV_STEP_CHUNKS or any(chunks):
        if rnd % WKV_STAGE_LAG == 0 and len(chunks) < WKV_STEP_CHUNKS:
            chunks.append(start_chunk(len(chunks)))
        chunks = [[gen for gen in groups if next(gen, "done") != "done"] for groups in chunks]
        rnd += 1


def _wkv(logw, a, k, v, r, p, prec_inv=None, prec_state=None):
    n = k.shape[0]
    blk = WKV_STEP_CHUNKS * WKV_CHUNK
    assert n % blk == 0 and CTX_LEN % blk == 0
    nb = n // blk
    nbc = CTX_LEN // blk
    nbx = nb - nbc

    def chunk(d, j):
        fwd = jnp.where(j < nbc, nbx + j, j - nbc)
        return jnp.where(d == 0, fwd, nb - 1 - j)

    dir_spec = pl.BlockSpec((1, blk, D_MODEL), lambda d, j: (d, chunk(d, j), 0))
    row_spec = pl.BlockSpec((blk, D_MODEL), lambda d, j: (chunk(d, j), 0))
    vecs = [p['k_k'].reshape(1, D_MODEL), p['k_a'].reshape(1, D_MODEL),
            p['r_k'].reshape(1, D_MODEL), p['ln_w'].reshape(1, D_MODEL),
            p['ln_b'].reshape(1, D_MODEL)]
    return pl.pallas_call(
        functools.partial(_wkv_body, prec_inv=prec_inv, prec_state=prec_state),
        grid=(2, nb),
        in_specs=[dir_spec, dir_spec, row_spec, row_spec, row_spec] + [_full(x) for x in vecs],
        out_specs=dir_spec,
        out_shape=jax.ShapeDtypeStruct((2, n, D_MODEL), BF16),
        scratch_shapes=[pltpu.VMEM((WKV_GROUPS, WKV_ROWS, LANES), F32)],
        compiler_params=_cparams(("arbitrary", "arbitrary")),
        name="wkv_scan",
    )(logw, a, k, v, r, *vecs)


def _out_body(*refs, gated):
    if gated:
        o_ref, g_ref, w_ref, s_ref, mod_ref, gpost_ref, out_ref = refs
        y = (o_ref[0].astype(F32) + o_ref[1].astype(F32)) * g_ref[...].astype(F32)
    else:
        o_ref, w_ref, s_ref, mod_ref, gpost_ref, out_ref = refs
        y = o_ref[...]
    y = jnp.dot(y.astype(BF16), w_ref[...], preferred_element_type=F32)
    out_ref[...] = _post(s_ref[...], y, mod_ref, 1, gpost_ref, 1.0)


def _mixer_out(o, g, w_o, s, mod, gpost):
    n = s.shape[0]
    gated = g is not None
    if gated:
        args = [o, g, w_o, s, mod, gpost]
        specs = [pl.BlockSpec((2, ROW_TILE, D_MODEL), lambda i: (0, i, 0)), _row_spec(),
                 _full(w_o), _row_spec(), _mod_spec(n), _full(gpost)]
    else:
        args = [o, w_o, s, mod, gpost]
        specs = [_row_spec(), _full(w_o), _row_spec(), _mod_spec(n), _full(gpost)]
    return pl.pallas_call(
        functools.partial(_out_body, gated=gated),
        grid=(n // ROW_TILE,),
        in_specs=specs,
        out_specs=_row_spec(),
        out_shape=jax.ShapeDtypeStruct(s.shape, F32),
        compiler_params=_cparams(("arbitrary",)),
        name="mixer_out",
    )(*args)


def _mla_proj_body(s_ref, mod_ref, gpre_ref, wdq_ref, qn_ref, wq1_ref, wq2_ref, wdkv_ref, kvn_ref,
                   wkr_ref, wk_ref, wv_ref, place_ref, tq_ref, tk_ref, q_ref, k_ref, v_ref):
    h = _pre(s_ref[...], mod_ref, 1, gpre_ref).astype(BF16)
    cq = jnp.dot(h, wdq_ref[...], preferred_element_type=F32)
    qn = _rms(cq, qn_ref[...]).astype(BF16)
    q1 = jnp.dot(qn, wq1_ref[...], preferred_element_type=F32)
    q2 = jnp.dot(qn, wq2_ref[...], preferred_element_type=F32)
    ckv = jnp.dot(h, wdkv_ref[...], preferred_element_type=F32)
    kvn = _rms(ckv, kvn_ref[...]).astype(BF16)
    kr = jnp.dot(h, wkr_ref[...], preferred_element_type=F32)
    krr = kr * tk_ref[0] + pltpu.roll(kr, LANES - MLA_ROPE, 1) * tk_ref[1]
    kk = (jnp.dot(kvn, wk_ref[...], preferred_element_type=F32)
          + jnp.dot(krr.astype(BF16), place_ref[...], preferred_element_type=F32))
    k_ref[...] = kk.astype(BF16)
    v_ref[0] = jnp.dot(kvn, wv_ref[...], preferred_element_type=F32).T.astype(BF16)
    cos = tq_ref[0]
    sin = tq_ref[1]
    for hd in range(MLA_HEADS):
        sl = slice(hd * MLA_QK_PAD, (hd + 1) * MLA_QK_PAD)
        q_ref[:, sl] = (q1[:, sl] * cos + q2[:, sl] * sin).astype(BF16)


def _rot_cols(w):
    half = MLA_ROPE // 4
    parts = []
    for ax in range(2):
        blk = w[..., ax * 2 * half:(ax + 1) * 2 * half]
        parts += [-blk[..., half:], blk[..., :half]]
    return jnp.concatenate(parts, axis=-1)


def _mla_tables(n):
    t = n - CTX_LEN
    pos = jnp.arange(t, dtype=jnp.int32)
    rowp = (pos // GRID_W).astype(F32)
    colp = (pos % GRID_W).astype(F32)
    axis_dim = MLA_ROPE // 2
    inv = ROPE_BASE ** (-jnp.arange(0, axis_dim, 2, dtype=F32) / axis_dim)
    ar = rowp[:, None] * inv
    ac = colp[:, None] * inv
    cosf = jnp.concatenate([jnp.cos(ar), jnp.cos(ar), jnp.cos(ac), jnp.cos(ac)], axis=-1)
    sinf = jnp.concatenate([jnp.sin(ar), jnp.sin(ar), jnp.sin(ac), jnp.sin(ac)], axis=-1)
    cosf = jnp.concatenate([cosf, jnp.ones((CTX_LEN, MLA_ROPE), F32)], axis=0)
    sinf = jnp.concatenate([sinf, jnp.zeros((CTX_LEN, MLA_ROPE), F32)], axis=0)
    scale = (MLA_NOPE + MLA_ROPE) ** -0.5 * math.log2(math.e)
    zq = jnp.zeros((n, MLA_QK_PAD - MLA_NOPE - MLA_ROPE), F32)
    tq = jnp.stack([jnp.concatenate([jnp.ones((n, MLA_NOPE), F32), cosf, zq], axis=-1),
                    jnp.concatenate([jnp.zeros((n, MLA_NOPE), F32), sinf, zq], axis=-1)]) * scale
    zk = jnp.zeros((n, LANES - MLA_ROPE), F32)
    tk = jnp.stack([jnp.concatenate([cosf, zk], axis=-1), jnp.concatenate([sinf, zk], axis=-1)])
    return tq, tk


def _mla_proj(s, mod, gpre, w_dq, q_norm, w_uq, w_dkv, kv_norm, w_ukv, key_tile):
    n = s.shape[0]
    per = key_tile // ROW_TILE
    hq = MLA_NOPE + MLA_ROPE
    wq = w_uq.reshape(MLA_Q_RANK, MLA_HEADS, hq)
    zq = jnp.zeros((MLA_Q_RANK, MLA_HEADS, MLA_QK_PAD - hq), w_uq.dtype)
    wq1 = jnp.concatenate([wq, zq], axis=-1).reshape(MLA_Q_RANK, MLA_HEADS * MLA_QK_PAD)
    wq2 = jnp.concatenate([jnp.zeros_like(wq[..., :MLA_NOPE]), _rot_cols(wq[..., MLA_NOPE:]), zq],
                          axis=-1).reshape(MLA_Q_RANK, MLA_HEADS * MLA_QK_PAD)
    wkv = w_ukv.reshape(MLA_KV_RANK, MLA_HEADS, MLA_NOPE + MLA_V)
    wk = jnp.concatenate([wkv[..., :MLA_NOPE],
                          jnp.zeros((MLA_KV_RANK, MLA_HEADS, MLA_QK_PAD - MLA_NOPE), w_ukv.dtype)],
                         axis=-1).reshape(MLA_KV_RANK, MLA_HEADS * MLA_QK_PAD)
    wv = wkv[..., MLA_NOPE:].reshape(MLA_KV_RANK, MLA_HEADS * MLA_V)
    w_rope = w_dkv[:, MLA_KV_RANK:]
    wkr = jnp.concatenate([w_rope, _rot_cols(w_rope),
                           jnp.zeros((D_MODEL, LANES - 2 * MLA_ROPE), w_dkv.dtype)], axis=-1)
    src = jnp.arange(LANES)[:, None]
    dst = jnp.arange(MLA_HEADS * MLA_QK_PAD)[None, :]
    place = jnp.logical_and(src < MLA_ROPE, dst % MLA_QK_PAD == MLA_NOPE + src).astype(BF16)
    tq, tk = _mla_tables(n)
    weights = [w_dq.astype(BF16), q_norm.reshape(1, MLA_Q_RANK), wq1.astype(BF16), wq2.astype(BF16),
               w_dkv[:, :MLA_KV_RANK].astype(BF16), kv_norm.reshape(1, MLA_KV_RANK),
               wkr.astype(BF16), wk.astype(BF16), wv.astype(BF16), place]
    tab_spec = pl.BlockSpec((2, ROW_TILE, LANES), lambda i: (0, i, 0))
    return pl.pallas_call(
        _mla_proj_body,
        grid=(n // ROW_TILE,),
        in_specs=[_row_spec(), _mod_spec(n), _full(gpre)] + [_full(w) for w in weights]
                 + [tab_spec, tab_spec],
        out_specs=[_row_spec(MLA_HEADS * MLA_QK_PAD), _row_spec(MLA_HEADS * MLA_QK_PAD),
                   pl.BlockSpec((1, MLA_HEADS * MLA_V, ROW_TILE), lambda i: (i // per, 0, i % per))],
        out_shape=[jax.ShapeDtypeStruct((n, MLA_HEADS * MLA_QK_PAD), BF16),
                   jax.ShapeDtypeStruct((n, MLA_HEADS * MLA_QK_PAD), BF16),
                   jax.ShapeDtypeStruct((n // key_tile, MLA_HEADS * MLA_V, key_tile), BF16)],
        compiler_params=_cparams(("arbitrary",)),
        name="mla_proj",
    )(s, mod, gpre, *weights, tq, tk)


MLA_PAIR = LANES // MLA_V


ATTN_KEY_BLOCK = 16
ATTN_SAFE_LOG2 = 90.0


def _attn_body(q_ref, k_ref, vt_ref, o_ref, m_ref, l_ref, acc_ref, s_ref, p_ref, kb_ref, *, tk, nk):
    tq = q_ref.shape[0]
    l_ref[...] = jnp.zeros_like(l_ref)
    acc_ref[...] = jnp.zeros_like(acc_ref)

    def row_norm2_max(x):
        xf = x.astype(F32)
        return jnp.max(jnp.sum(xf * xf, axis=-1, keepdims=True), axis=0, keepdims=True)

    @pl.when(pl.program_id(1) == 0)
    def _():
        for hd in range(MLA_PAIR):
            sl = slice(hd * MLA_QK_PAD, (hd + 1) * MLA_QK_PAD)

            def key_tile(t, best):
                rows = pl.ds(pl.multiple_of(t * tk, tk), tk)
                return jnp.maximum(best, row_norm2_max(k_ref[rows, sl]))

            kb_ref[hd] = lax.fori_loop(0, nk, key_tile, jnp.zeros((1, 1), F32))

    bound2 = jnp.zeros((1, 1), F32)
    for hd in range(MLA_PAIR):
        sl = slice(hd * MLA_QK_PAD, (hd + 1) * MLA_QK_PAD)
        bound2 = jnp.maximum(bound2, row_norm2_max(q_ref[:, sl]) * kb_ref[hd])
    small = bound2[0, 0] <= ATTN_SAFE_LOG2 ** 2

    @pl.when(small)
    def _():
        def tile(t, carry):
            rows = pl.ds(pl.multiple_of(t * tk, tk), tk)
            for hd in range(MLA_PAIR):
                sl = slice(hd * MLA_QK_PAD, (hd + 1) * MLA_QK_PAD)
                s = lax.dot_general(k_ref[rows, sl], q_ref[:, sl], (((1,), (1,)), ((), ())),
                                    preferred_element_type=F32)
                p = jnp.exp2(s)
                l_ref[hd] += jnp.sum(p.reshape(tk // SUBLANES, SUBLANES, tq), axis=0)
                vt = vt_ref[t, hd * MLA_V:(hd + 1) * MLA_V, :]
                acc_ref[hd] += jnp.dot(vt, p.astype(BF16), preferred_element_type=F32)
            return carry

        lax.fori_loop(0, nk, tile, 0)

    @pl.when(jnp.logical_not(small))
    def _():
        _attn_running_max(q_ref, k_ref, vt_ref, m_ref, l_ref, acc_ref, s_ref, p_ref, tk=tk, nk=nk)

    outs = [acc_ref[hd] / jnp.sum(l_ref[hd], axis=0, keepdims=True) for hd in range(MLA_PAIR)]
    o_ref[...] = jnp.concatenate(outs, axis=0).T.astype(o_ref.dtype)


def _attn_running_max(q_ref, k_ref, vt_ref, m_ref, l_ref, acc_ref, s_ref, p_ref, *, tk, nk):
    tq = q_ref.shape[0]
    m_ref[...] = jnp.full_like(m_ref, -jnp.inf)

    def tile(t, carry):
        rows = pl.ds(pl.multiple_of(t * tk, tk), tk)
        for hd in range(MLA_PAIR):
            sl = slice(hd * MLA_QK_PAD, (hd + 1) * MLA_QK_PAD)
            s_ref[hd] = lax.dot_general(k_ref[rows, sl], q_ref[:, sl], (((1,), (1,)), ((), ())),
                                        preferred_element_type=F32)
        for hd in range(MLA_PAIR):
            mx = s_ref[hd, 0:SUBLANES, :]
            for j in range(1, tk // SUBLANES):
                mx = jnp.maximum(mx, s_ref[hd, j * SUBLANES:(j + 1) * SUBLANES, :])
            m_old = m_ref[hd]
            m_new = jnp.maximum(m_old, jnp.max(mx, axis=0, keepdims=True))
            m_ref[hd] = m_new
            alpha = jnp.exp2(m_old - m_new)
            m_rows = jnp.broadcast_to(m_new, (ATTN_KEY_BLOCK, tq))
            lsum = alpha * l_ref[hd]
            for b in range(tk // ATTN_KEY_BLOCK):
                r = slice(b * ATTN_KEY_BLOCK, (b + 1) * ATTN_KEY_BLOCK)
                p = jnp.exp2(s_ref[hd, r, :] - m_rows)
                lsum = lsum + (p[:SUBLANES] + p[SUBLANES:])
                p_ref[hd, r, :] = p.astype(BF16)
            l_ref[hd] = lsum
            vt = vt_ref[t, hd * MLA_V:(hd + 1) * MLA_V, :]
            acc_ref[hd] = alpha * acc_ref[hd] + jnp.dot(vt, p_ref[hd], preferred_element_type=F32)
        return carry

    lax.fori_loop(0, nk, tile, 0)


def _attention(q, k, vt, q_row0, q_rows, k_row0, k_rows, tq, tk):
    vtile = vt.shape[2]
    assert q_row0 % tq == 0 and q_rows % tq == 0 and k_row0 % k_rows == 0 and k_rows % tk == 0
    q0 = q_row0 // tq
    k0 = k_row0 // k_rows
    pair_w = MLA_PAIR * MLA_QK_PAD
    if k_rows % vtile == 0:
        assert tk == vtile
        vt_spec = pl.BlockSpec((k_rows // vtile, LANES, vtile), lambda p, i: (k_row0 // k_rows, p, 0))
    else:
        assert vtile % k_rows == 0 and tk == k_rows
        vt_spec = pl.BlockSpec((1, LANES, k_rows),
                               lambda p, i: (k_row0 // vtile, p, (k_row0 % vtile) // k_rows))
    return pl.pallas_call(
        functools.partial(_attn_body, tk=tk, nk=k_rows // tk),
        grid=(MLA_HEADS // MLA_PAIR, q_rows // tq),
        in_specs=[pl.BlockSpec((tq, pair_w), lambda p, i: (i + q0, p)),
                  pl.BlockSpec((k_rows, pair_w), lambda p, i: (k0, p)),
                  vt_spec],
        out_specs=pl.BlockSpec((tq, LANES), lambda p, i: (i, p)),
        out_shape=jax.ShapeDtypeStruct((q_rows, MLA_HEADS * MLA_V), BF16),
        scratch_shapes=[pltpu.VMEM((MLA_PAIR, 1, tq), F32), pltpu.VMEM((MLA_PAIR, SUBLANES, tq), F32),
                        pltpu.VMEM((MLA_PAIR, MLA_V, tq), F32), pltpu.VMEM((MLA_PAIR, tk, tq), F32),
                        pltpu.VMEM((MLA_PAIR, tk, tq), BF16), pltpu.VMEM((MLA_PAIR, 1, 1), F32)],
        compiler_params=_cparams(("arbitrary", "arbitrary")),
        name="mla_attention",
    )(q, k, vt)


def _largest_divisor(n, candidates):
    for c in candidates:
        if n % c == 0:
            return c
    raise ValueError(f"no tile in {candidates} divides {n}")


def _pool_body(s_ref, sp_ref, sn_ref, mod_ref, gpre_ref, gpost_ref, w_ref, b_ref, sc_ref, o_ref):
    i = pl.program_id(0)
    nt = pl.num_programs(0)
    nct = CTX_LEN // ROW_TILE
    first, last = _stream_edges()
    s = s_ref[...]
    h = _pre(s, mod_ref, 1, gpre_ref)
    hp = jnp.where(first, 0.0, _pre(sp_ref[...], mod_ref, 1, gpre_ref))
    hn = jnp.where(last, 0.0, _pre(sn_ref[...], mod_ref, 1, gpre_ref))
    ext = jnp.concatenate([hp, h, hn], axis=0)
    er = ROW_TILE + 2 * POOL_HALO
    in_ctx = i >= nt - nct
    t0 = jnp.where(in_ctx, i - (nt - nct), i) * ROW_TILE
    t_len = jnp.where(in_ctx, nct, nt - nct) * ROW_TILE
    t = t0 + lax.broadcasted_iota(jnp.int32, (ROW_TILE, 1), 0)

    outs = []
    run = ext
    width = 1
    for gi, win in enumerate(POOL_WINDOWS):
        while width < win:
            run = run + pltpu.roll(run, width, 0)
            width *= 2
        ahead = win // 2 - 1
        grp = run[:, :POOL_GROUP]
        if ahead:
            grp = pltpu.roll(grp, er - ahead, 0)
        wsum = grp[POOL_HALO:POOL_HALO + ROW_TILE]
        cnt = (jnp.minimum(t + win // 2, t_len) - jnp.maximum(t - win // 2, 0)).astype(F32)
        diff = wsum / cnt - h[:, gi * POOL_GROUP:(gi + 1) * POOL_GROUP]
        outs.append(_bdot(diff, w_ref[gi]) + b_ref[gi:gi + 1, :])
        run = run[:, POOL_GROUP:]
    y = jnp.concatenate(outs, axis=-1) * sc_ref[...]
    o_ref[...] = _post(s, y, mod_ref, 1, gpost_ref, 1.0)


def _pool(s, mod, gpre, gpost, w, b, scale):
    n = s.shape[0]
    prev_spec, next_spec = _halo_specs(n)
    sc = scale.reshape(1, D_MODEL)
    wb = w.astype(BF16)
    return pl.pallas_call(
        _pool_body,
        grid=(n // ROW_TILE,),
        in_specs=[_row_spec(), prev_spec, next_spec, _mod_spec(n), _full(gpre), _full(gpost),
                  _full(wb), _full(b), _full(sc)],
        out_specs=_row_spec(),
        out_shape=jax.ShapeDtypeStruct(s.shape, F32),
        compiler_params=_cparams(("arbitrary",)),
        name="pool_mixer",
    )(s, s, s, mod, gpre, gpost, wb, b, sc)


def kernel(x, c, ctx, c_ctx, mod_w, mod_b, norm_pre, norm_post, ffn_w_gate, ffn_w_up, ffn_w_down,
           rw_mu, rw_w_r, rw_w_k, rw_w_v, rw_w_o, rw_w0, rw_w1, rw_w2, rw_a0, rw_a1, rw_a2,
           rw_v0, rw_v1, rw_v2, rw_g1, rw_g2, rw_k_k, rw_k_a, rw_r_k, rw_ln_w, rw_ln_b,
           mla_w_dq, mla_q_norm, mla_w_uq, mla_w_dkv, mla_kv_norm, mla_w_ukv, mla_w_o,
           pool_w, pool_b, pool_scale):
    batch, t, d = x.shape
    assert batch == 1 and d == D_MODEL and ctx.shape == (1, CTX_LEN, D_MODEL)
    assert t % ROW_TILE == 0 and CTX_LEN % ROW_TILE == 0 and t % GRID_W == 0
    n = t + CTX_LEN
    s = jnp.concatenate([x[0], ctx[0]], axis=0)
    c2 = jnp.concatenate([c_ctx[None], c, jnp.zeros((SUBLANES - 2, D_MODEL), F32)], axis=0)
    mod_all = _modulation(c2, mod_w, mod_b)
    wg_all, wu_all, wd_all = (w.astype(BF16) for w in (ffn_w_gate, ffn_w_up, ffn_w_down))
    v_first = None
    for i in range(DEPTH):
        kind, j = i % 3, i // 3
        mod = mod_all[i]
        gpre, gpost = norm_pre[i], norm_post[i]
        s = _ffn(s, mod, gpre, gpost, wg_all, wu_all, wd_all, i, 0, 0, n)
        if kind == 0:
            p = {'mu': rw_mu[j], 'w_r': rw_w_r[j], 'w_k': rw_w_k[j], 'w_v': rw_w_v[j],
                 'w0': rw_w0[j], 'w1': rw_w1[j], 'w2': rw_w2[j],
                 'a0': rw_a0[j], 'a1': rw_a1[j], 'a2': rw_a2[j],
                 'g1': rw_g1[j], 'g2': rw_g2[j], 'k_k': rw_k_k[j], 'k_a': rw_k_a[j],
                 'r_k': rw_r_k[j], 'ln_w': rw_ln_w[j], 'ln_b': rw_ln_b[j]}
            if j > 0:
                p['v0'], p['v1'], p['v2'] = rw_v0[j - 1], rw_v1[j - 1], rw_v2[j - 1]
            r, k, v, g, logw, a = _rwkv_proj(s, mod, gpre, p, v_first if j > 0 else None)
            if j == 0:
                v_first = v
            o = _wkv(logw, a, k, v, r, p)
            s = _mixer_out(o, g, rw_w_o[j].astype(BF16), s, mod, gpost)
        elif kind == 1:
            tq = _largest_divisor(t, (1024, 512, 256))
            tk = _largest_divisor(n, (1280, 256))
            q, k, vt = _mla_proj(s, mod, gpre, mla_w_dq[j], mla_q_norm[j], mla_w_uq[j],
                                 mla_w_dkv[j], mla_kv_norm[j], mla_w_ukv[j], tk)
            o_x = _attention(q, k, vt, 0, t, 0, n, tq, tk)
            o_c = _attention(q, k, vt, t, CTX_LEN, t, CTX_LEN, CTX_LEN, CTX_LEN)
            o = jnp.concatenate([o_x, o_c], axis=0)
            s = _mixer_out(o, None, mla_w_o[j].astype(BF16), s, mod, gpost)
        else:
            s = _pool(s, mod, gpre, gpost, pool_w[j], pool_b[j], pool_scale[j])
        s = _ffn(s, mod, gpre, gpost, wg_all, wu_all, wd_all, i, 1, 2, t if i == DEPTH - 1 else n)
    return s[None]
```
